```python
import math, functools
import jax, jax.numpy as jnp
from jax import lax
import numpy as np

D_MODEL = 1024
BATCH = 8
SEQ = 2048
DEPTH = 4
DEC_BATCH = 128
DEC_SEQ = 4
PAST_LEN = 2048
PAGE_SIZE = 128

N_META = 16
SB_HEADS = 8
SB_HEAD_DIM = 64
SB_WIDTH = SB_HEADS * SB_HEAD_DIM
Q_BLOCK = 128
SB_BIAS_INIT = -6.0
SSM_WIDTH = D_MODEL // 2
SSM_GROUP = 16
SSM_GROUPS = SSM_WIDTH // SSM_GROUP
SSM_STATE = 64
DT_MIN = 1e-3
DT_MAX = 1e-1
CONV_WIDTH = D_MODEL // 2
CONV_K = 3
D_FF = 4 * D_MODEL
N_BRANCH = 3
PROJ_WIDTH = 3 * SB_WIDTH + SSM_WIDTH + 3 * CONV_WIDTH + N_BRANCH * D_MODEL
DN_ALPHA = (2 * DEPTH) ** 0.25
DN_BETA = (8 * DEPTH) ** -0.25
LN_EPS = 1e-5

kernel_name = 'hybrid_sb_s5_shortconv_decoder_step'


def layer_norm(x, g, b):
    xf = x.astype(jnp.float32)
    mu = jnp.mean(xf, axis=-1, keepdims=True)
    var = jnp.mean(jnp.square(xf - mu), axis=-1, keepdims=True)
    return ((xf - mu) * lax.rsqrt(var + LN_EPS) * g.astype(jnp.float32) + b.astype(jnp.float32)).astype(x.dtype)


def sb_attend(q, k, v, q_pos, k_pos, bias):
    z = jnp.einsum('bqhd,bkhd->bhqk', q, k, preferred_element_type=jnp.float32) * (SB_HEAD_DIM ** -0.5)
    z = z + bias.astype(jnp.float32)[None, :, None, None]
    mask = k_pos[None, :] < q_pos[:, None]
    log_keep = jnp.where(mask, jax.nn.log_sigmoid(-z), 0.0)
    suffix = lax.cumsum(log_keep, axis=3, reverse=True) - log_keep
    w = jnp.where(mask, jnp.exp(jax.nn.log_sigmoid(z) + suffix), 0.0)
    return jnp.einsum('bhqk,bkhd->bqhd', w.astype(v.dtype), v)


def prompt_attend(q, k, v, bias):
    bsz, t = q.shape[0], q.shape[1]
    pos = jnp.arange(t, dtype=jnp.int32)
    o_meta = sb_attend(q[:, :N_META], k, v, pos[:N_META], pos, bias)
    nb = (t - N_META) // Q_BLOCK
    qb = q[:, N_META:].reshape(bsz, nb, Q_BLOCK, SB_HEADS, SB_HEAD_DIM).transpose(1, 0, 2, 3, 4)
    pb = pos[N_META:].reshape(nb, Q_BLOCK)
    ob = lax.map(lambda a: sb_attend(a[0], k, v, a[1], pos, bias), (qb, pb))
    ob = ob.transpose(1, 0, 2, 3, 4).reshape(bsz, nb * Q_BLOCK, SB_HEADS, SB_HEAD_DIM)
    return jnp.concatenate([o_meta, ob], axis=1)


def paged_attend(q, k, v, bias, pool_k, pool_v, page_table):
    dbs, s = q.shape[0], q.shape[1]
    past_k = pool_k[page_table].reshape(dbs, -1, SB_HEADS, SB_HEAD_DIM).astype(k.dtype)
    past_v = pool_v[page_table].reshape(dbs, -1, SB_HEADS, SB_HEAD_DIM).astype(v.dtype)
    past = past_k.shape[1]
    k_all = jnp.concatenate([past_k, k], axis=1)
    v_all = jnp.concatenate([past_v, v], axis=1)
    k_pos = jnp.arange(past + s, dtype=jnp.int32)
    q_pos = past + jnp.arange(s, dtype=jnp.int32)
    return sb_attend(q, k_all, v_all, q_pos, k_pos, bias)


def cmul(ar, ai, br, bi):
    return ar * br - ai * bi, ar * bi + ai * br


def s5_scan(u, h0_re, h0_im, lam_re, lam_im, log_dt, b_re, b_im, c_re, c_im, d_skip):
    f32 = jnp.float32
    bsz, t, _ = u.shape
    uf = u.astype(f32)
    ug = uf.reshape(bsz, t, SSM_GROUPS, SSM_GROUP)
    lr, li = lam_re.astype(f32), lam_im.astype(f32)
    dt = jnp.exp(log_dt.astype(f32))[:, None]
    mag = jnp.exp(lr * dt)
    ab_re, ab_im = mag * jnp.cos(li * dt), mag * jnp.sin(li * dt)
    den = lr * lr + li * li
    nr, ni = ab_re - 1.0, ab_im
    fr, fi = (nr * lr + ni * li) / den, (ni * lr - nr * li) / den
    bb_re, bb_im = cmul(fr[..., None], fi[..., None], b_re.astype(f32), b_im.astype(f32))
    bu_re = jnp.einsum('btgp,gnp->tbgn', ug, bb_re)
    bu_im = jnp.einsum('btgp,gnp->tbgn', ug, bb_im)
    a_re = jnp.broadcast_to(ab_re[None, None], (t, 1, SSM_GROUPS, SSM_STATE))
    a_im = jnp.broadcast_to(ab_im[None, None], (t, 1, SSM_GROUPS, SSM_STATE))

    def combine(e1, e2):
        a1r, a1i, b1r, b1i = e1
        a2r, a2i, b2r, b2i = e2
        ar, ai = cmul(a2r, a2i, a1r, a1i)
        br, bi = cmul(a2r, a2i, b1r, b1i)
        return ar, ai, br + b2r, bi + b2i

    pr, pim, sr, si = lax.associative_scan(combine, (a_re, a_im, bu_re, bu_im), axis=0)
    hr0, hi0 = cmul(pr, pim, h0_re.astype(f32)[None], h0_im.astype(f32)[None])
    hr, hi = sr + hr0, si + hi0
    y = (jnp.einsum('tbgn,gpn->btgp', hr, c_re.astype(f32))
         - jnp.einsum('tbgn,gpn->btgp', hi, c_im.astype(f32)))
    y = y.reshape(bsz, t, SSM_WIDTH) + d_skip.astype(f32) * uf
    return y, hr[-1], hi[-1]


def short_conv(v, buf, w):
    t = v.shape[1]
    cat = jnp.concatenate([buf.astype(v.dtype), v], axis=1)
    y = w[0] * cat[:, 0:t]
    for i in range(1, CONV_K):
        y = y + w[i] * cat[:, i:i + t]
    return y, cat[:, -(CONV_K - 1):]


def layer_forward(x, attend, h0_re, h0_im, conv_buf, w_in, sb_bias, lam_re, lam_im, log_dt, b_re, b_im,
                  c_re, c_im, d_skip, w_glu, conv_w, w_branch_a, w_branch_c, w_out, ln1_g, ln1_b,
                  w_up, w_down, ln2_g, ln2_b):
    bsz, t, _ = x.shape
    proj = x @ w_in
    sizes = (SB_WIDTH,) * 3 + (SSM_WIDTH,) + (CONV_WIDTH,) * 3 + (D_MODEL,) * N_BRANCH
    idx = [int(i) for i in np.cumsum(sizes)[:-1]]
    q, k, v, u, h_in, cb, cc, g_att, g_ssm, g_conv = jnp.split(proj, idx, axis=-1)
    q = q.reshape(bsz, t, SB_HEADS, SB_HEAD_DIM)
    k = k.reshape(bsz, t, SB_HEADS, SB_HEAD_DIM)
    v = v.reshape(bsz, t, SB_HEADS, SB_HEAD_DIM)
    y_att = attend(q, k, v, sb_bias).reshape(bsz, t, SB_WIDTH) @ w_branch_a
    y_s, hT_re, hT_im = s5_scan(u, h0_re, h0_im, lam_re, lam_im, log_dt, b_re, b_im, c_re, c_im, d_skip)
    zg = jax.nn.gelu(y_s).astype(x.dtype) @ w_glu
    y_ssm = zg[..., :D_MODEL] * jax.nn.sigmoid(zg[..., D_MODEL:])
    conv_y, new_buf = short_conv(cc * h_in, conv_buf, conv_w)
    y_conv = (cb * conv_y) @ w_branch_c
    mix = jax.nn.sigmoid(g_att) * y_att + jax.nn.sigmoid(g_ssm) * y_ssm + jax.nn.sigmoid(g_conv) * y_conv
    x = layer_norm(DN_ALPHA * x + mix @ w_out, ln1_g, ln1_b)
    ff = jnp.square(jax.nn.relu(x @ w_up)) @ w_down
    x = layer_norm(DN_ALPHA * x + ff, ln2_g, ln2_b)
    return x, k, v, hT_re, hT_im, new_buf


def setup_inputs(seed: int = 0) -> dict:
    key = jax.random.key(seed)
    ks = jax.random.split(key, 40)
    f32 = jnp.float32
    n_pages = PAST_LEN // PAGE_SIZE
    n_used = DEC_BATCH * n_pages
    n_phys = (n_used * 5) // 4

    def nrm(k, shape, s):
        return jax.random.normal(k, shape, f32) * s

    n_idx = jnp.arange(SSM_STATE, dtype=f32)
    gs = (DEPTH, SSM_GROUPS, SSM_STATE)
    return {
        'x_prompt': nrm(ks[0], (BATCH, SEQ, D_MODEL), 1.0),
        'x_sample': nrm(ks[1], (DEC_BATCH, DEC_SEQ, D_MODEL), 1.0),
        'cache_k': nrm(ks[2], (DEPTH, n_phys, PAGE_SIZE, SB_HEADS, SB_HEAD_DIM), 1.0),
        'cache_v': nrm(ks[3], (DEPTH, n_phys, PAGE_SIZE, SB_HEADS, SB_HEAD_DIM), 1.0),
        'state_ssm_re': nrm(ks[4], (DEPTH, DEC_BATCH, SSM_GROUPS, SSM_STATE), 0.1),
        'state_ssm_im': nrm(ks[5], (DEPTH, DEC_BATCH, SSM_GROUPS, SSM_STATE), 0.1),
        'state_conv': nrm(ks[6], (DEPTH, DEC_BATCH, CONV_K - 1, CONV_WIDTH), 1.0),
        'page_table': jax.random.permutation(ks[7], n_phys)[:n_used].reshape(DEC_BATCH, n_pages).astype(jnp.int32),
        'meta_tokens': nrm(ks[8], (N_META, D_MODEL), 1.0),
        'ln_in_g': 1.0 + nrm(ks[9], (D_MODEL,), 0.02),
        'ln_in_b': nrm(ks[10], (D_MODEL,), 0.02),
        'w_in': nrm(ks[11], (DEPTH, D_MODEL, PROJ_WIDTH), D_MODEL ** -0.5),
        'sb_bias': SB_BIAS_INIT + nrm(ks[31], (DEPTH, SB_HEADS), 0.1),
        'ssm_lam_re': -0.5 + nrm(ks[12], gs, 0.01),
        'ssm_lam_im': math.pi * n_idx + nrm(ks[13], gs, 0.01),
        'ssm_log_dt': jax.random.uniform(ks[14], (DEPTH, SSM_GROUPS), f32, math.log(DT_MIN), math.log(DT_MAX)),
        'ssm_b_re': nrm(ks[15], (DEPTH, SSM_GROUPS, SSM_STATE, SSM_GROUP), (2 * SSM_GROUP) ** -0.5),
        'ssm_b_im': nrm(ks[16], (DEPTH, SSM_GROUPS, SSM_STATE, SSM_GROUP), (2 * SSM_GROUP) ** -0.5),
        'ssm_c_re': nrm(ks[17], (DEPTH, SSM_GROUPS, SSM_GROUP, SSM_STATE), SSM_STATE ** -0.5),
        'ssm_c_im': nrm(ks[18], (DEPTH, SSM_GROUPS, SSM_GROUP, SSM_STATE), SSM_STATE ** -0.5),
        'ssm_d': nrm(ks[19], (DEPTH, SSM_WIDTH), 1.0),
        'w_glu': nrm(ks[20], (DEPTH, SSM_WIDTH, 2 * D_MODEL), SSM_WIDTH ** -0.5),
        'conv_w': nrm(ks[21], (DEPTH, CONV_K, CONV_WIDTH), CONV_K ** -0.5),
        'w_branch_a': nrm(ks[22], (DEPTH, SB_WIDTH, D_MODEL), SB_WIDTH ** -0.5),
        'w_branch_c': nrm(ks[23], (DEPTH, CONV_WIDTH, D_MODEL), CONV_WIDTH ** -0.5),
        'w_out': nrm(ks[24], (DEPTH, D_MODEL, D_MODEL), D_MODEL ** -0.5 * DN_BETA),
        'ln1_g': 1.0 + nrm(ks[25], (DEPTH, D_MODEL), 0.02),
        'ln1_b': nrm(ks[26], (DEPTH, D_MODEL), 0.02),
        'w_up': nrm(ks[27], (DEPTH, D_MODEL, D_FF), D_MODEL ** -0.5),
        'w_down': nrm(ks[28], (DEPTH, D_FF, D_MODEL), D_FF ** -0.5 * DN_BETA),
        'ln2_g': 1.0 + nrm(ks[29], (DEPTH, D_MODEL), 0.02),
        'ln2_b': nrm(ks[30], (DEPTH, D_MODEL), 0.02),
    }


def reference(x_prompt, x_sample, cache_k, cache_v, state_ssm_re, state_ssm_im, state_conv, page_table,
              meta_tokens, ln_in_g, ln_in_b, w_in, sb_bias, ssm_lam_re, ssm_lam_im, ssm_log_dt, ssm_b_re,
              ssm_b_im, ssm_c_re, ssm_c_im, ssm_d, w_glu, conv_w, w_branch_a, w_branch_c, w_out, ln1_g, ln1_b,
              w_up, w_down, ln2_g, ln2_b):
    cdt = x_prompt.dtype
    bsz = x_prompt.shape[0]
    meta = jnp.broadcast_to(meta_tokens.astype(cdt)[None], (bsz, N_META, D_MODEL))
    xp = layer_norm(jnp.concatenate([meta, x_prompt], axis=1), ln_in_g, ln_in_b)
    xs = layer_norm(x_sample, ln_in_g, ln_in_b)
    h0_zero = jnp.zeros((bsz, SSM_GROUPS, SSM_STATE), jnp.float32)
    buf_zero = jnp.zeros((bsz, CONV_K - 1, CONV_WIDTH), cdt)
    kp, vp, ks_, vs_, srp, sip, srs, sis, cvp, cvs = ([] for _ in range(10))
    for l in range(DEPTH):
        w = (w_in[l], sb_bias[l], ssm_lam_re[l], ssm_lam_im[l], ssm_log_dt[l], ssm_b_re[l], ssm_b_im[l],
             ssm_c_re[l], ssm_c_im[l], ssm_d[l], w_glu[l], conv_w[l], w_branch_a[l], w_branch_c[l],
             w_out[l], ln1_g[l], ln1_b[l], w_up[l], w_down[l], ln2_g[l], ln2_b[l])
        xp, k1, v1, r1, i1, c1 = layer_forward(xp, prompt_attend, h0_zero, h0_zero, buf_zero, *w)
        att_s = functools.partial(paged_attend, pool_k=cache_k[l], pool_v=cache_v[l], page_table=page_table)
        xs, k2, v2, r2, i2, c2 = layer_forward(xs, att_s, state_ssm_re[l], state_ssm_im[l], state_conv[l], *w)
        kp.append(k1); vp.append(v1); srp.append(r1); sip.append(i1); cvp.append(c1)
        ks_.append(k2); vs_.append(v2); srs.append(r2); sis.append(i2); cvs.append(c2)
    y_prompt = xp[:, N_META:]
    return (y_prompt, xs, jnp.stack(kp), jnp.stack(vp), jnp.stack(ks_), jnp.stack(vs_),
            jnp.stack(srp), jnp.stack(sip), jnp.stack(srs), jnp.stack(sis), jnp.stack(cvp), jnp.stack(cvs))
```

```python
import functools
import math

import jax
import jax.numpy as jnp
from jax import lax
from jax.experimental import pallas as pl
from jax.experimental.pallas import tpu as pltpu

F32 = jnp.float32
BF16 = jnp.bfloat16

N_LAYERS = 4
N_META_TOK = 16
N_HEADS = 8
HEAD_DIM = 64
ATT_WIDTH = N_HEADS * HEAD_DIM
SSM_W = 512
SSM_GRP = 16
SSM_NGRP = SSM_W // SSM_GRP
SSM_NSTATE = 64
CONV_W = 512
CONV_TAPS = 3
PAGE = 128
ALPHA = (2 * N_LAYERS) ** 0.25
EPS = 1e-5
QK_SCALE = HEAD_DIM ** -0.5

MAIN_W = 3 * ATT_WIDTH + SSM_W + 3 * CONV_W
LANE = 128
KBLK = 128
SSM_SLICES = 4
SSM_SL_IN = SSM_W // SSM_SLICES
SSM_SL_ST = SSM_NGRP * SSM_NSTATE // SSM_SLICES
VMEM_CAP = 56 * 1024 * 1024


def _cparams(sem, vmem_mb):
    return pltpu.CompilerParams(dimension_semantics=sem,
                                vmem_limit_bytes=min(vmem_mb * 1024 * 1024, VMEM_CAP))


def _layer_norm(xf, g, b):
    mu = jnp.mean(xf, axis=-1, keepdims=True)
    xc = xf - mu
    var = jnp.mean(xc * xc, axis=-1, keepdims=True)
    return xc * lax.rsqrt(var + EPS) * g + b


def _sigmoid(x):
    return 1.0 / (1.0 + jnp.exp(-x))


def _gelu_tanh(x):
    c = math.sqrt(2.0 / math.pi)
    return 0.5 * x * (1.0 + jnp.tanh(c * (x + 0.044715 * (x * x * x))))


def _softplus(z):
    return jnp.maximum(z, 0.0) + jnp.log1p(jnp.exp(-jnp.abs(z)))


def _dot(a, b):
    return jnp.dot(a, b, preferred_element_type=F32)


def _dot_nt(a, b):
    return lax.dot_general(a, b, (((1,), (1,)), ((), ())), preferred_element_type=F32)


def _split_bf16(x):
    hi = x.astype(BF16)
    lo = (x - hi.astype(F32)).astype(BF16)
    return hi, lo


def _ln_kernel(x_ref, g_ref, b_ref, o_ref):
    o_ref[...] = _layer_norm(x_ref[...], g_ref[...], b_ref[...])


def layernorm_rows(x, g, b, tm):
    n, d = x.shape
    return pl.pallas_call(
        _ln_kernel,
        grid=(n // tm,),
        in_specs=[pl.BlockSpec((tm, d), lambda i: (i, 0)),
                  pl.BlockSpec((1, d), lambda i: (0, 0)),
                  pl.BlockSpec((1, d), lambda i: (0, 0))],
        out_specs=pl.BlockSpec((tm, d), lambda i: (i, 0)),
        out_shape=jax.ShapeDtypeStruct((n, d), F32),
        compiler_params=_cparams(("parallel",), 32),
        name="layernorm_rows",
    )(x, g.reshape(1, d), b.reshape(1, d))


def _mm_kernel(x_ref, w_ref, o_ref):
    o_ref[...] = _dot(x_ref[...].astype(BF16), w_ref[...])


def matmul_xw(x, w, tm, tn):
    m, k = x.shape
    n = w.shape[1]
    return pl.pallas_call(
        _mm_kernel,
        grid=(n // tn, m // tm),
        in_specs=[pl.BlockSpec((tm, k), lambda j, i: (i, 0)),
                  pl.BlockSpec((k, tn), lambda j, i: (0, j))],
        out_specs=pl.BlockSpec((tm, tn), lambda j, i: (i, j)),
        out_shape=jax.ShapeDtypeStruct((m, n), F32),
        compiler_params=_cparams(("parallel", "parallel"), 40),
        name="matmul_xw",
    )(x, w)


def _ssm_prep_kernel(lr_ref, li_ref, ldt_ref, br_ref, bi_ref, abr_ref, abi_ref, bbr_ref, bbi_ref):
    lr, li = lr_ref[...], li_ref[...]
    dt = jnp.exp(ldt_ref[...])
    mag = jnp.exp(lr * dt)
    ab_re = mag * jnp.cos(li * dt)
    ab_im = mag * jnp.sin(li * dt)
    den = lr * lr + li * li
    nr, ni = ab_re - 1.0, ab_im
    fr = (nr * lr + ni * li) / den
    fi = (ni * lr - nr * li) / den
    abr_ref[...] = ab_re
    abi_ref[...] = ab_im
    br, bi = br_ref[...], bi_ref[...]
    bbr_ref[...] = fr * br - fi * bi
    bbi_ref[...] = fr * bi + fi * br


def ssm_prep(lam_re, lam_im, log_dt, b_re, b_im):
    lg = lam_re.shape[0] * lam_re.shape[1]
    n, p = SSM_NSTATE, SSM_GRP
    args = (lam_re.reshape(lg, 1, n), lam_im.reshape(lg, 1, n), log_dt.reshape(lg, 1, 1),
            jnp.swapaxes(b_re, -1, -2).reshape(lg, p, n), jnp.swapaxes(b_im, -1, -2).reshape(lg, p, n))
    full = lambda a: pl.BlockSpec(a.shape, lambda i: (0,) * a.ndim)
    outs = (jax.ShapeDtypeStruct((lg, 1, n), F32),) * 2 + (jax.ShapeDtypeStruct((lg, p, n), F32),) * 2
    return pl.pallas_call(
        _ssm_prep_kernel,
        grid=(1,),
        in_specs=[full(a) for a in args],
        out_specs=tuple(pl.BlockSpec(o.shape, lambda i: (0, 0, 0)) for o in outs),
        out_shape=outs,
        compiler_params=_cparams(("arbitrary",), 32),
        name="ssm_prep",
    )(*args)


def _block_diag_slices(m):
    g, r, c = m.shape
    per = g // SSM_SLICES
    m4 = m.reshape(SSM_SLICES, per, r, c)
    eye = jnp.eye(per, dtype=m.dtype)
    return jnp.einsum('sjrc,jk->sjrkc', m4, eye).reshape(SSM_SLICES, per * r, per * c)


def _ssm_prompt_kernel(u_ref, bbr_ref, bbi_ref, cr_ref, ci_ref, abr_ref, abi_ref, d_ref,
                       h0r_ref, h0i_ref, y_ref, htr_ref, hti_ref, bur, bui, hr_s, hi_s, *, nb, tc):
    j = pl.program_id(1)

    @pl.when(j == 0)
    def _():
        hr_s[...] = h0r_ref[...]
        hi_s[...] = h0i_ref[...]

    u = u_ref[...].reshape(nb * tc, SSM_SL_IN)
    ub = u.astype(BF16)
    nlb = SSM_SL_ST // LANE
    bu_r = _dot(ub, bbr_ref[...].astype(BF16))
    bu_i = _dot(ub, bbi_ref[...].astype(BF16))
    for c in range(nlb):
        bur[c] = bu_r[:, c * LANE:(c + 1) * LANE]
        bui[c] = bu_i[:, c * LANE:(c + 1) * LANE]
    ar = jnp.broadcast_to(abr_ref[...], (nb, SSM_SL_ST))
    ai = jnp.broadcast_to(abi_ref[...], (nb, SSM_SL_ST))

    def step(t, carry):
        hr, hi = carry
        rows = pl.ds(t, nb, stride=tc)
        in_r = jnp.concatenate([bur[c, rows, :] for c in range(nlb)], axis=1)
        in_i = jnp.concatenate([bui[c, rows, :] for c in range(nlb)], axis=1)
        nhr = ar * hr - ai * hi + in_r
        nhi = ar * hi + ai * hr + in_i
        for c in range(nlb):
            bur[c, rows, :] = nhr[:, c * LANE:(c + 1) * LANE]
            bui[c, rows, :] = nhi[:, c * LANE:(c + 1) * LANE]
        return nhr, nhi

    hr, hi = lax.fori_loop(0, tc, step, (hr_s[...], hi_s[...]))
    hr_s[...] = hr
    hi_s[...] = hi
    htr_ref[...] = hr
    hti_ref[...] = hi
    hr_all = jnp.concatenate([bur[c] for c in range(nlb)], axis=1)
    hi_all = jnp.concatenate([bui[c] for c in range(nlb)], axis=1)
    y = (_dot(hr_all.astype(BF16), cr_ref[...].astype(BF16))
         - _dot(hi_all.astype(BF16), ci_ref[...].astype(BF16)) + d_ref[...] * u)
    y_ref[...] = _gelu_tanh(y).reshape(nb, tc, SSM_SL_IN)


def ssm_prompt(proj3, u_col0, bbr, bbi, cr, ci, abr, abi, dsk, h0r, h0i, tc):
    nb, t, _ = proj3.shape
    cb0 = u_col0 // SSM_SL_IN
    sl = lambda shape: pl.BlockSpec((None,) + shape, lambda s, j: (s, 0, 0))
    kern = functools.partial(_ssm_prompt_kernel, nb=nb, tc=tc)
    return pl.pallas_call(
        kern,
        grid=(SSM_SLICES, t // tc),
        in_specs=[pl.BlockSpec((nb, tc, SSM_SL_IN), lambda s, j: (0, j, cb0 + s)),
                  sl((SSM_SL_IN, SSM_SL_ST)), sl((SSM_SL_IN, SSM_SL_ST)),
                  sl((SSM_SL_ST, SSM_SL_IN)), sl((SSM_SL_ST, SSM_SL_IN)),
                  sl((1, SSM_SL_ST)), sl((1, SSM_SL_ST)), sl((1, SSM_SL_IN)),
                  sl((nb, SSM_SL_ST)), sl((nb, SSM_SL_ST))],
        out_specs=(pl.BlockSpec((nb, tc, SSM_SL_IN), lambda s, j: (0, j, s)),
                   sl((nb, SSM_SL_ST)), sl((nb, SSM_SL_ST))),
        out_shape=(jax.ShapeDtypeStruct((nb, t, SSM_W), F32),
                   jax.ShapeDtypeStruct((SSM_SLICES, nb, SSM_SL_ST), F32),
                   jax.ShapeDtypeStruct((SSM_SLICES, nb, SSM_SL_ST), F32)),
        scratch_shapes=[pltpu.VMEM((SSM_SL_ST // LANE, nb * tc, LANE), F32),
                        pltpu.VMEM((SSM_SL_ST // LANE, nb * tc, LANE), F32),
                        pltpu.VMEM((nb, SSM_SL_ST), F32), pltpu.VMEM((nb, SSM_SL_ST), F32)],
        compiler_params=_cparams(("parallel", "arbitrary"), 48),
        name="ssm_prompt",
    )(proj3, bbr, bbi, cr, ci, abr, abi, dsk, h0r, h0i)


def _ssm_sample_kernel(u_ref, bbr_ref, bbi_ref, cr_ref, ci_ref, abr_ref, abi_ref, d_ref,
                       h0r_ref, h0i_ref, y_ref, htr_ref, hti_ref, *, nt):
    bbr_hi, bbr_lo = _split_bf16(bbr_ref[...])
    bbi_hi, bbi_lo = _split_bf16(bbi_ref[...])
    crb = cr_ref[...].astype(BF16)
    cib = ci_ref[...].astype(BF16)
    ar, ai = abr_ref[...], abi_ref[...]
    hr, hi = h0r_ref[...], h0i_ref[...]
    for t in range(nt):
        u = u_ref[t]
        u_hi, u_lo = _split_bf16(u)
        bu_r = _dot(u_hi, bbr_hi) + (_dot(u_hi, bbr_lo) + _dot(u_lo, bbr_hi))
        bu_i = _dot(u_hi, bbi_hi) + (_dot(u_hi, bbi_lo) + _dot(u_lo, bbi_hi))
        hr, hi = ar * hr - ai * hi + bu_r, ar * hi + ai * hr + bu_i
        y = _dot(hr.astype(BF16), crb) - _dot(hi.astype(BF16), cib) + d_ref[...] * u
        y_ref[t] = _gelu_tanh(y)
    htr_ref[...] = hr
    hti_ref[...] = hi


def ssm_sample(proj3, u_col0, bbr, bbi, cr, ci, abr, abi, dsk, h0r, h0i):
    nt, nb, _ = proj3.shape
    cb0 = u_col0 // SSM_SL_IN
    sl = lambda shape: pl.BlockSpec((None,) + shape, lambda s: (s, 0, 0))
    kern = functools.partial(_ssm_sample_kernel, nt=nt)
    return pl.pallas_call(
        kern,
        grid=(SSM_SLICES,),
        in_specs=[pl.BlockSpec((nt, nb, SSM_SL_IN), lambda s: (0, 0, cb0 + s)),
                  sl((SSM_SL_IN, SSM_SL_ST)), sl((SSM_SL_IN, SSM_SL_ST)),
                  sl((SSM_SL_ST, SSM_SL_IN)), sl((SSM_SL_ST, SSM_SL_IN)),
                  sl((1, SSM_SL_ST)), sl((1, SSM_SL_ST)), sl((1, SSM_SL_IN)),
                  sl((nb, SSM_SL_ST)), sl((nb, SSM_SL_ST))],
        out_specs=(pl.BlockSpec((nt, nb, SSM_SL_IN), lambda s: (0, 0, s)),
                   sl((nb, SSM_SL_ST)), sl((nb, SSM_SL_ST))),
        out_shape=(jax.ShapeDtypeStruct((nt, nb, SSM_W), F32),
                   jax.ShapeDtypeStruct((SSM_SLICES, nb, SSM_SL_ST), F32),
                   jax.ShapeDtypeStruct((SSM_SLICES, nb, SSM_SL_ST), F32)),
        compiler_params=_cparams(("parallel",), 32),
        name="ssm_sample",
    )(proj3, bbr, bbi, cr, ci, abr, abi, dsk, h0r, h0i)


def _conv_prompt_kernel(h_ref, cb_ref, cc_ref, w_ref, buf_ref, cv_ref, nbuf_ref, cat, *, t):
    pad = 8
    cat[pad - 2:pad, :] = buf_ref[...]
    cat[pad:pad + t, :] = cc_ref[...] * h_ref[...]
    w = w_ref[...]
    y = (w[0:1] * cat[pad - 2:pad - 2 + t, :] + w[1:2] * cat[pad - 1:pad - 1 + t, :]
         + w[2:3] * cat[pad:pad + t, :])
    cv_ref[...] = cb_ref[...] * y
    nbuf_ref[...] = cat[pad + t - 2:pad + t, :]


def conv_prompt(proj3, h_col0, conv_w, buf):
    nb, t, _ = proj3.shape
    c0 = h_col0 // LANE
    nblk = CONV_W // LANE
    kern = functools.partial(_conv_prompt_kernel, t=t)
    col = lambda off: pl.BlockSpec((None, t, LANE), lambda b, c: (b, 0, c0 + off + c))
    return pl.pallas_call(
        kern,
        grid=(nb, nblk),
        in_specs=[col(0), col(nblk), col(2 * nblk),
                  pl.BlockSpec((CONV_TAPS, LANE), lambda b, c: (0, c)),
                  pl.BlockSpec((None, CONV_TAPS - 1, LANE), lambda b, c: (b, 0, c))],
        out_specs=(pl.BlockSpec((None, t, LANE), lambda b, c: (b, 0, c)),
                   pl.BlockSpec((None, CONV_TAPS - 1, LANE), lambda b, c: (b, 0, c))),
        out_shape=(jax.ShapeDtypeStruct((nb, t, CONV_W), F32),
                   jax.ShapeDtypeStruct((nb, CONV_TAPS - 1, CONV_W), F32)),
        scratch_shapes=[pltpu.VMEM((t + 8, LANE), F32)],
        compiler_params=_cparams(("parallel", "parallel"), 32),
        name="conv_prompt",
    )(proj3, proj3, proj3, conv_w, buf)


def _conv_sample_kernel(h_ref, cb_ref, cc_ref, w_ref, buf_ref, cv_ref, nbuf_ref, *, nt):
    w = w_ref[...]
    rows = [buf_ref[0], buf_ref[1]] + [cc_ref[t] * h_ref[t] for t in range(nt)]
    for t in range(nt):
        y = w[0:1] * rows[t] + w[1:2] * rows[t + 1] + w[2:3] * rows[t + 2]
        cv_ref[t] = cb_ref[t] * y
    nbuf_ref[0] = rows[nt]
    nbuf_ref[1] = rows[nt + 1]


def conv_sample(proj3, h_col0, conv_w, buf_t):
    nt, nb, _ = proj3.shape
    c0 = h_col0 // CONV_W
    kern = functools.partial(_conv_sample_kernel, nt=nt)
    col = lambda off: pl.BlockSpec((nt, nb, CONV_W), lambda i: (0, 0, c0 + off))
    return pl.pallas_call(
        kern,
        grid=(1,),
        in_specs=[col(0), col(1), col(2),
                  pl.BlockSpec((CONV_TAPS, CONV_W), lambda i: (0, 0)),
                  pl.BlockSpec((CONV_TAPS - 1, nb, CONV_W), lambda i: (0, 0, 0))],
        out_specs=(pl.BlockSpec((nt, nb, CONV_W), lambda i: (0, 0, 0)),
                   pl.BlockSpec((CONV_TAPS - 1, nb, CONV_W), lambda i: (0, 0, 0))),
        out_shape=(jax.ShapeDtypeStruct((nt, nb, CONV_W), F32),
                   jax.ShapeDtypeStruct((CONV_TAPS - 1, nb, CONV_W), F32)),
        compiler_params=_cparams(("arbitrary",), 32),
        name="conv_sample",
    )(proj3, proj3, proj3, conv_w, buf_t)


def _suffix_matrices(n):
    r = jnp.arange(n)[:, None]
    c = jnp.arange(n + LANE)[None, :]
    return jnp.where((r > c) | (c >= n), 1.0, 0.0).astype(BF16)


def _att_prompt_kernel(bias_ref, q_ref, k_ref, v_ref, maug_ref, mtail_ref, o_ref, acc, car, *, t):
    hp = pl.program_id(1)
    b0 = bias_ref[2 * hp]
    b1 = bias_ref[2 * hp + 1]
    nfull = t // KBLK
    tail = t - nfull * KBLK

    def qblock(row0, nq, n_prev):
        lane = lax.broadcasted_iota(jnp.int32, (nq, LANE), 1)
        q = q_ref[pl.ds(row0, nq), :] * QK_SCALE
        q2 = jnp.concatenate([jnp.where(lane < HEAD_DIM, q, 0.0),
                              jnp.where(lane >= HEAD_DIM, q, 0.0)], axis=0).astype(BF16)
        row = lax.broadcasted_iota(jnp.int32, (2 * nq, 1), 0)
        bias2 = jnp.where(row < nq, b0, b1)
        kd = k_ref[pl.ds(row0, nq), :].astype(BF16)
        vd = v_ref[pl.ds(row0, nq), :].astype(BF16)
        z = _dot_nt(q2, kd) + bias2
        ri = lax.broadcasted_iota(jnp.int32, (2 * nq, nq), 0)
        ci = lax.broadcasted_iota(jnp.int32, (2 * nq, nq), 1)
        vis = ci < jnp.where(ri >= nq, ri - nq, ri)
        sp = _softplus(z)
        lk = jnp.where(vis, -sp, 0.0)
        hi, lo = _split_bf16(lk)
        mref = maug_ref if nq == KBLK else mtail_ref
        s_aug = _dot(hi, mref[...]) + _dot(lo, mref[...])
        w = jnp.where(vis, jnp.exp(z - sp + s_aug[:, :nq]), 0.0)
        acc[0:2 * nq, :] = _dot(w.astype(BF16), vd)
        car[0:2 * nq, :] = s_aug[:, nq:]

        def body(i, carry):
            kb = n_prev - 1 - i
            r = pl.multiple_of(kb * KBLK, KBLK)
            kk = k_ref[pl.ds(r, KBLK), :].astype(BF16)
            vv = v_ref[pl.ds(r, KBLK), :].astype(BF16)
            z = _dot_nt(q2, kk) + bias2
            sp = _softplus(z)
            hi, lo = _split_bf16(-sp)
            s_aug = _dot(hi, maug_ref[...]) + _dot(lo, maug_ref[...])
            c = car[0:2 * nq, :]
            w = jnp.exp(z - sp + s_aug[:, :KBLK] + c)
            acc[0:2 * nq, :] += _dot(w.astype(BF16), vv)
            car[0:2 * nq, :] = c + s_aug[:, KBLK:]
            return carry

        lax.fori_loop(0, n_prev, body, 0)
        o2 = acc[0:2 * nq, :]
        o_ref[pl.ds(row0, nq), :] = jnp.where(lane < HEAD_DIM, o2[0:nq], o2[nq:2 * nq])

    def outer(i, carry):
        qblock(pl.multiple_of(i * KBLK, KBLK), KBLK, i)
        return carry

    lax.fori_loop(0, nfull, outer, 0)
    if tail:
        qblock(nfull * KBLK, tail, nfull)


def attention_prompt(proj3, bias):
    nb, t, _ = proj3.shape
    tail = t % KBLK
    npair = N_HEADS // 2
    col = lambda off: pl.BlockSpec((None, t, LANE), lambda b, h: (b, 0, off + h))
    maug = _suffix_matrices(KBLK)
    mtail = _suffix_matrices(tail if tail else 8)
    kern = functools.partial(_att_prompt_kernel, t=t)
    return pl.pallas_call(
        kern,
        grid=(nb, npair),
        in_specs=[pl.BlockSpec(memory_space=pltpu.SMEM),
                  col(0), col(npair), col(2 * npair),
                  pl.BlockSpec(maug.shape, lambda b, h: (0, 0)),
                  pl.BlockSpec(mtail.shape, lambda b, h: (0, 0))],
        out_specs=pl.BlockSpec((None, t, LANE), lambda b, h: (b, 0, h)),
        out_shape=jax.ShapeDtypeStruct((nb, t, ATT_WIDTH), F32),
        scratch_shapes=[pltpu.VMEM((2 * KBLK, LANE), F32), pltpu.VMEM((2 * KBLK, LANE), F32)],
        compiler_params=_cparams(("parallel", "parallel"), 32),
        name="attention_prompt",
    )(bias, proj3, proj3, proj3, maug, mtail)


def _att_paged_kernel(pt_ref, q_ref, kn_ref, vn_ref, bias_ref, maug_ref, *refs, npages, ns):
    del pt_ref
    k_refs = refs[:npages]
    v_refs = refs[npages:2 * npages]
    o_ref = refs[2 * npages]
    nr = ns * N_HEADS
    q = q_ref[...] * QK_SCALE
    lane = lax.broadcasted_iota(jnp.int32, (N_HEADS, ATT_WIDTH), 1)
    hrow = lax.broadcasted_iota(jnp.int32, (N_HEADS, ATT_WIDTH), 0)
    hmask = (lane >= hrow * HEAD_DIM) & (lane < (hrow + 1) * HEAD_DIM)
    wq = jnp.concatenate(
        [jnp.where(hmask, jnp.broadcast_to(q[i:i + 1, :], (N_HEADS, ATT_WIDTH)), 0.0) for i in range(ns)],
        axis=0)
    wqb = wq.astype(BF16)
    bias = jnp.concatenate([bias_ref[...]] * ns, axis=0)
    bias1 = bias[:, 0:1]
    qidx = lax.broadcasted_iota(jnp.int32, (nr, 1), 0) // N_HEADS

    kn = kn_ref[...]
    vn = vn_ref[...]
    zs, sps, lks, vis = [], [], [], []
    for j in range(ns):
        zj = jnp.sum(wq * kn[j:j + 1, :], axis=-1, keepdims=True) + bias1
        vj = qidx > j
        spj = _softplus(zj)
        zs.append(zj); sps.append(spj); vis.append(vj)
        lks.append(jnp.where(vj, -spj, 0.0))
    o = jnp.zeros((nr, ATT_WIDTH), F32)
    suf = jnp.zeros((nr, 1), F32)
    for j in reversed(range(ns)):
        wj = jnp.where(vis[j], jnp.exp(zs[j] - sps[j] + suf), 0.0)
        o = o + wj * vn[j:j + 1, :]
        suf = suf + lks[j]
    car = jnp.broadcast_to(suf, (nr, LANE))

    for p in reversed(range(npages)):
        kt = k_refs[p][...].astype(BF16)
        vt = v_refs[p][...].astype(BF16)
        z = _dot(wqb, kt) + bias
        sp = _softplus(z)
        hi, lo = _split_bf16(-sp)
        s_aug = _dot(hi, maug_ref[...]) + _dot(lo, maug_ref[...])
        w = jnp.exp(z - sp + s_aug[:, :KBLK] + car)
        o = o + _dot_nt(w.astype(BF16), vt)
        car = car + s_aug[:, KBLK:]

    o3 = o.reshape(ns, N_HEADS, ATT_WIDTH)
    o_ref[...] = jnp.sum(jnp.where(hmask[None], o3, 0.0), axis=1)


def attention_paged(q, kn, vn, bias, cache_k, cache_v, layer, page_table):
    nb, ns, _ = q.shape
    npages = page_table.shape[1]
    maug = _suffix_matrices(KBLK)
    bias_rep = jnp.broadcast_to(bias[:, None], (N_HEADS, LANE))
    row = pl.BlockSpec((None, ns, ATT_WIDTH), lambda b, pt: (b, 0, 0))

    def page_spec(p):
        return pl.BlockSpec((None, None, ATT_WIDTH, PAGE), lambda b, pt: (layer, pt[b, p], 0, 0))

    kern = functools.partial(_att_paged_kernel, npages=npages, ns=ns)
    return pl.pallas_call(
        kern,
        grid_spec=pltpu.PrefetchScalarGridSpec(
            num_scalar_prefetch=1,
            grid=(nb,),
            in_specs=[row, row, row,
                      pl.BlockSpec((N_HEADS, LANE), lambda b, pt: (0, 0)),
                      pl.BlockSpec(maug.shape, lambda b, pt: (0, 0))]
                     + [page_spec(p) for p in range(npages)] * 2,
            out_specs=row),
        out_shape=jax.ShapeDtypeStruct((nb, ns, ATT_WIDTH), F32),
        compiler_params=_cparams(("arbitrary",), 40),
        name="attention_paged",
    )(page_table, q, kn, vn, bias_rep, maug, *([cache_k] * npages), *([cache_v] * npages))


def _mix_kernel(x_ref, att_ref, ssm_ref, cv_ref, gate_ref, wa_ref, wg_ref, wc_ref, wo_ref,
                g_ref, b_ref, o_ref, *, d):
    y_att = _dot(att_ref[...].astype(BF16), wa_ref[...])
    zg = _dot(ssm_ref[...].astype(BF16), wg_ref[...])
    y_ssm = zg[:, :d] * _sigmoid(zg[:, d:])
    y_conv = _dot(cv_ref[...].astype(BF16), wc_ref[...])
    mix = (_sigmoid(gate_ref[:, 0:d]) * y_att + _sigmoid(gate_ref[:, d:2 * d]) * y_ssm
           + _sigmoid(gate_ref[:, 2 * d:3 * d]) * y_conv)
    r = ALPHA * x_ref[...] + _dot(mix.astype(BF16), wo_ref[...])
    o_ref[...] = _layer_norm(r, g_ref[...], b_ref[...])


def mix_block(x, att, ssm, cv, gates, wa, wg, wc, wo, g, b, tm):
    n, d = x.shape
    rows = lambda w: pl.BlockSpec((tm, w), lambda i: (i, 0))
    full = lambda a: pl.BlockSpec(a.shape, lambda i: (0, 0))
    g2, b2 = g.reshape(1, d), b.reshape(1, d)
    kern = functools.partial(_mix_kernel, d=d)
    return pl.pallas_call(
        kern,
        grid=(n // tm,),
        in_specs=[rows(d), rows(att.shape[1]), rows(ssm.shape[1]), rows(cv.shape[1]), rows(gates.shape[1]),
                  full(wa), full(wg), full(wc), full(wo), full(g2), full(b2)],
        out_specs=rows(d),
        out_shape=jax.ShapeDtypeStruct((n, d), F32),
        compiler_params=_cparams(("parallel",), 52),
        name="mix_block",
    )(x, att, ssm, cv, gates, wa, wg, wc, wo, g2, b2)


def _mlp_kernel(x_ref, wu_ref, wd_ref, g_ref, b_ref, o_ref, acc_ref, xb_ref):
    c = pl.program_id(1)

    @pl.when(c == 0)
    def _():
        acc_ref[...] = jnp.zeros_like(acc_ref)
        xb_ref[...] = x_ref[...].astype(BF16)

    h = jnp.maximum(_dot(xb_ref[...], wu_ref[...]), 0.0)
    acc_ref[...] += _dot((h * h).astype(BF16), wd_ref[...])

    @pl.when(c == pl.num_programs(1) - 1)
    def _():
        o_ref[...] = _layer_norm(ALPHA * x_ref[...] + acc_ref[...], g_ref[...], b_ref[...])


def mlp_block(x, wu, wd, g, b, tm, tf):
    n, d = x.shape
    f = wu.shape[1]
    return pl.pallas_call(
        _mlp_kernel,
        grid=(n // tm, f // tf),
        in_specs=[pl.BlockSpec((tm, d), lambda i, c: (i, 0)),
                  pl.BlockSpec((d, tf), lambda i, c: (0, c)),
                  pl.BlockSpec((tf, d), lambda i, c: (c, 0)),
                  pl.BlockSpec((1, d), lambda i, c: (0, 0)),
                  pl.BlockSpec((1, d), lambda i, c: (0, 0))],
        out_specs=pl.BlockSpec((tm, d), lambda i, c: (i, 0)),
        out_shape=jax.ShapeDtypeStruct((n, d), F32),
        scratch_shapes=[pltpu.VMEM((tm, d), F32), pltpu.VMEM((tm, d), BF16)],
        compiler_params=_cparams(("parallel", "arbitrary"), 48),
        name="mlp_block",
    )(x, wu, wd, g.reshape(1, d), b.reshape(1, d))


def _row_tile(n, target):
    best = 8
    for c in range(8, target + 1, 8):
        if n % c == 0:
            best = c
    return best


def _slice_state(h):
    nb = h.shape[0]
    return h.reshape(nb, SSM_SLICES, SSM_SL_ST).transpose(1, 0, 2)


def _unslice_state(h):
    nb = h.shape[1]
    return h.transpose(1, 0, 2).reshape(nb, SSM_NGRP, SSM_NSTATE)


def kernel(x_prompt, x_sample, cache_k, cache_v, state_ssm_re, state_ssm_im, state_conv, page_table, meta_tokens, ln_in_g, ln_in_b, w_in, sb_bias, ssm_lam_re, ssm_lam_im, ssm_log_dt, ssm_b_re, ssm_b_im, ssm_c_re, ssm_c_im, ssm_d, w_glu, conv_w, w_branch_a, w_branch_c, w_out, ln1_g, ln1_b, w_up, w_down, ln2_g, ln2_b):
    nb, seq, d = x_prompt.shape
    sb, ss, _ = x_sample.shape
    t = seq + N_META_TOK
    n_p, n_s = nb * t, sb * ss
    depth = w_in.shape[0]

    w_main = w_in[:, :, :MAIN_W].astype(BF16)
    w_gate = w_in[:, :, MAIN_W:].astype(BF16)
    wa, wg, wc, wo = (w.astype(BF16) for w in (w_branch_a, w_glu, w_branch_c, w_out))
    wu, wd = w_up.astype(BF16), w_down.astype(BF16)

    abr, abi, bbr_t, bbi_t = ssm_prep(ssm_lam_re, ssm_lam_im, ssm_log_dt, ssm_b_re, ssm_b_im)
    abr = abr.reshape(depth, SSM_SLICES, 1, SSM_SL_ST)
    abi = abi.reshape(depth, SSM_SLICES, 1, SSM_SL_ST)
    bbr_t = bbr_t.reshape(depth, SSM_NGRP, SSM_GRP, SSM_NSTATE)
    bbi_t = bbi_t.reshape(depth, SSM_NGRP, SSM_GRP, SSM_NSTATE)
    dsk = ssm_d.reshape(depth, SSM_SLICES, 1, SSM_SL_IN)
    cache_k4 = cache_k.transpose(0, 1, 3, 4, 2).reshape(cache_k.shape[0], cache_k.shape[1], ATT_WIDTH, PAGE)
    cache_v4 = cache_v.transpose(0, 1, 3, 4, 2).reshape(cache_v.shape[0], cache_v.shape[1], ATT_WIDTH, PAGE)

    tm_p = _row_tile(n_p, 688)
    tm_mix = _row_tile(n_p, 344)
    tm_s = _row_tile(n_s, 256)
    tc = _row_tile(t, 344)

    meta = jnp.broadcast_to(meta_tokens.astype(x_prompt.dtype)[None], (nb, N_META_TOK, d))
    xp = layernorm_rows(jnp.concatenate([meta, x_prompt], axis=1).reshape(n_p, d), ln_in_g, ln_in_b, tm_p)
    xs = layernorm_rows(x_sample.transpose(1, 0, 2).reshape(n_s, d), ln_in_g, ln_in_b, tm_s)

    h0_zero = jnp.zeros((SSM_SLICES, nb, SSM_SL_ST), F32)
    buf_zero = jnp.zeros((nb, CONV_TAPS - 1, CONV_W), F32)
    outs = [[] for _ in range(10)]
    u_col, h_col = 3 * ATT_WIDTH, 3 * ATT_WIDTH + SSM_W
    for l in range(depth):
        bbr = _block_diag_slices(bbr_t[l])
        bbi = _block_diag_slices(bbi_t[l])
        cr = _block_diag_slices(jnp.swapaxes(ssm_c_re[l], -1, -2))
        ci = _block_diag_slices(jnp.swapaxes(ssm_c_im[l], -1, -2))
        ssm_w = (bbr, bbi, cr, ci, abr[l], abi[l], dsk[l])

        main = matmul_xw(xp, w_main[l], tm_p, MAIN_W // 4).reshape(nb, t, MAIN_W)
        gates = matmul_xw(xp, w_gate[l], tm_p, (3 * d) // 4)
        att = attention_prompt(main, sb_bias[l])
        ysm, htr, hti = ssm_prompt(main, u_col, *ssm_w, h0_zero, h0_zero, tc)
        cv, nbuf = conv_prompt(main, h_col, conv_w[l], buf_zero)
        x1 = mix_block(xp, att.reshape(n_p, -1), ysm.reshape(n_p, -1), cv.reshape(n_p, -1), gates,
                       wa[l], wg[l], wc[l], wo[l], ln1_g[l], ln1_b[l], tm_mix)
        xp = mlp_block(x1, wu[l], wd[l], ln2_g[l], ln2_b[l], tm_p, 1024)
        outs[0].append(main[:, :, ATT_WIDTH:2 * ATT_WIDTH].reshape(nb, t, N_HEADS, HEAD_DIM))
        outs[1].append(main[:, :, 2 * ATT_WIDTH:3 * ATT_WIDTH].reshape(nb, t, N_HEADS, HEAD_DIM))
        outs[4].append(_unslice_state(htr))
        outs[5].append(_unslice_state(hti))
        outs[8].append(nbuf)

        main_s = matmul_xw(xs, w_main[l], tm_s, MAIN_W // 4).reshape(ss, sb, MAIN_W)
        gates_s = matmul_xw(xs, w_gate[l], tm_s, (3 * d) // 4)
        qkv = main_s[:, :, :3 * ATT_WIDTH].transpose(1, 0, 2)
        q_s, k_s, v_s = (qkv[:, :, i * ATT_WIDTH:(i + 1) * ATT_WIDTH] for i in range(3))
        att_s = attention_paged(q_s, k_s, v_s, sb_bias[l], cache_k4, cache_v4, l, page_table)
        att_s = att_s.transpose(1, 0, 2).reshape(n_s, ATT_WIDTH)
        ysm_s, htr_s, hti_s = ssm_sample(main_s, u_col, *ssm_w,
                                         _slice_state(state_ssm_re[l]), _slice_state(state_ssm_im[l]))
        cv_s, nbuf_s = conv_sample(main_s, h_col, conv_w[l], state_conv[l].transpose(1, 0, 2))
        x1_s = mix_block(xs, att_s, ysm_s.reshape(n_s, -1), cv_s.reshape(n_s, -1), gates_s,
                         wa[l], wg[l], wc[l], wo[l], ln1_g[l], ln1_b[l], tm_s)
        xs = mlp_block(x1_s, wu[l], wd[l], ln2_g[l], ln2_b[l], tm_s, 1024)
        outs[2].append(k_s.reshape(sb, ss, N_HEADS, HEAD_DIM))
        outs[3].append(v_s.reshape(sb, ss, N_HEADS, HEAD_DIM))
        outs[6].append(_unslice_state(htr_s))
        outs[7].append(_unslice_state(hti_s))
        outs[9].append(nbuf_s.transpose(1, 0, 2))

    y_prompt = xp.reshape(nb, t, d)[:, N_META_TOK:]
    y_sample = xs.reshape(ss, sb, d).transpose(1, 0, 2)
    return (y_prompt, y_sample) + tuple(jnp.stack(o) for o in outs)
```

```python
import functools
import math

import jax
import jax.numpy as jnp
from jax import lax
from jax.experimental import pallas as pl
from jax.experimental.pallas import tpu as pltpu

F32 = jnp.float32
BF16 = jnp.bfloat16

N_LAYERS = 4
N_META_TOK = 16
N_HEADS = 8
HEAD_DIM = 64
ATT_WIDTH = N_HEADS * HEAD_DIM
SSM_W = 512
SSM_GRP = 16
SSM_NGRP = SSM_W // SSM_GRP
SSM_NSTATE = 64
CONV_W = 512
CONV_TAPS = 3
PAGE = 128
ALPHA = (2 * N_LAYERS) ** 0.25
EPS = 1e-5
QK_SCALE = HEAD_DIM ** -0.5

MAIN_W = 3 * ATT_WIDTH + SSM_W + 3 * CONV_W
LANE = 128
KBLK = 128
SSM_SLICES = 4
SSM_SL_IN = SSM_W // SSM_SLICES
SSM_SL_ST = SSM_NGRP * SSM_NSTATE // SSM_SLICES
VMEM_CAP = 56 * 1024 * 1024


def _cparams(sem, vmem_mb):
    return pltpu.CompilerParams(dimension_semantics=sem,
                                vmem_limit_bytes=min(vmem_mb * 1024 * 1024, VMEM_CAP))


def _layer_norm(xf, g, b):
    mu = jnp.mean(xf, axis=-1, keepdims=True)
    xc = xf - mu
    var = jnp.mean(xc * xc, axis=-1, keepdims=True)
    return xc * lax.rsqrt(var + EPS) * g + b


def _sigmoid(x):
    return 1.0 / (1.0 + jnp.exp(-x))


def _gelu_tanh(x):
    c = math.sqrt(2.0 / math.pi)
    return 0.5 * x * (1.0 + jnp.tanh(c * (x + 0.044715 * (x * x * x))))


def _softplus(z):
    return jnp.maximum(z, 0.0) + jnp.log(1.0 + jnp.exp(-jnp.abs(z)))


def _dot(a, b):
    return jnp.dot(a, b, preferred_element_type=F32)


def _dot_nt(a, b):
    return lax.dot_general(a, b, (((1,), (1,)), ((), ())), preferred_element_type=F32)


def _split_bf16(x):
    hi = x.astype(BF16)
    lo = (x - hi.astype(F32)).astype(BF16)
    return hi, lo


def _ln_kernel(x_ref, g_ref, b_ref, o_ref):
    o_ref[...] = _layer_norm(x_ref[...], g_ref[...], b_ref[...])


def layernorm_rows(x, g, b, tm):
    n, d = x.shape
    return pl.pallas_call(
        _ln_kernel,
        grid=(n // tm,),
        in_specs=[pl.BlockSpec((tm, d), lambda i: (i, 0)),
                  pl.BlockSpec((1, d), lambda i: (0, 0)),
                  pl.BlockSpec((1, d), lambda i: (0, 0))],
        out_specs=pl.BlockSpec((tm, d), lambda i: (i, 0)),
        out_shape=jax.ShapeDtypeStruct((n, d), F32),
        compiler_params=_cparams(("parallel",), 32),
        name="layernorm_rows",
    )(x, g.reshape(1, d), b.reshape(1, d))


def _mm_kernel(x_ref, w_ref, o_ref):
    o_ref[...] = _dot(x_ref[...].astype(BF16), w_ref[...])


def matmul_xw(x, w, tm, tn):
    m, k = x.shape
    n = w.shape[1]
    return pl.pallas_call(
        _mm_kernel,
        grid=(n // tn, m // tm),
        in_specs=[pl.BlockSpec((tm, k), lambda j, i: (i, 0)),
                  pl.BlockSpec((k, tn), lambda j, i: (0, j))],
        out_specs=pl.BlockSpec((tm, tn), lambda j, i: (i, j)),
        out_shape=jax.ShapeDtypeStruct((m, n), F32),
        compiler_params=_cparams(("parallel", "parallel"), 40),
        name="matmul_xw",
    )(x, w)


def _ssm_prep_kernel(lr_ref, li_ref, ldt_ref, br_ref, bi_ref, abr_ref, abi_ref, bbr_ref, bbi_ref):
    lr, li = lr_ref[...], li_ref[...]
    dt = jnp.exp(ldt_ref[...])
    mag = jnp.exp(lr * dt)
    ab_re = mag * jnp.cos(li * dt)
    ab_im = mag * jnp.sin(li * dt)
    den = lr * lr + li * li
    nr, ni = ab_re - 1.0, ab_im
    fr = (nr * lr + ni * li) / den
    fi = (ni * lr - nr * li) / den
    abr_ref[...] = ab_re
    abi_ref[...] = ab_im
    br, bi = br_ref[...], bi_ref[...]
    bbr_ref[...] = fr * br - fi * bi
    bbi_ref[...] = fr * bi + fi * br


def ssm_prep(lam_re, lam_im, log_dt, b_re, b_im):
    lg = lam_re.shape[0] * lam_re.shape[1]
    n, p = SSM_NSTATE, SSM_GRP
    args = (lam_re.reshape(lg, 1, n), lam_im.reshape(lg, 1, n), log_dt.reshape(lg, 1, 1),
            jnp.swapaxes(b_re, -1, -2).reshape(lg, p, n), jnp.swapaxes(b_im, -1, -2).reshape(lg, p, n))
    full = lambda a: pl.BlockSpec(a.shape, lambda i: (0,) * a.ndim)
    outs = (jax.ShapeDtypeStruct((lg, 1, n), F32),) * 2 + (jax.ShapeDtypeStruct((lg, p, n), F32),) * 2
    return pl.pallas_call(
        _ssm_prep_kernel,
        grid=(1,),
        in_specs=[full(a) for a in args],
        out_specs=tuple(pl.BlockSpec(o.shape, lambda i: (0, 0, 0)) for o in outs),
        out_shape=outs,
        compiler_params=_cparams(("arbitrary",), 32),
        name="ssm_prep",
    )(*args)


def _block_diag_slices(m):
    g, r, c = m.shape
    per = g // SSM_SLICES
    m4 = m.reshape(SSM_SLICES, per, r, c)
    eye = jnp.eye(per, dtype=m.dtype)
    return jnp.einsum('sjrc,jk->sjrkc', m4, eye).reshape(SSM_SLICES, per * r, per * c)


def _ssm_prompt_kernel(u_ref, bbr_ref, bbi_ref, cr_ref, ci_ref, abr_ref, abi_ref, d_ref,
                       h0r_ref, h0i_ref, y_ref, htr_ref, hti_ref, bur, bui, hr_s, hi_s, *, nb, tc):
    j = pl.program_id(1)

    @pl.when(j == 0)
    def _():
        hr_s[...] = h0r_ref[...]
        hi_s[...] = h0i_ref[...]

    u = u_ref[...].reshape(nb * tc, SSM_SL_IN)
    ub = u.astype(BF16)
    nlb = SSM_SL_ST // LANE
    bu_r = _dot(ub, bbr_ref[...].astype(BF16))
    bu_i = _dot(ub, bbi_ref[...].astype(BF16))
    for c in range(nlb):
        bur[c] = bu_r[:, c * LANE:(c + 1) * LANE]
        bui[c] = bu_i[:, c * LANE:(c + 1) * LANE]
    ar = jnp.broadcast_to(abr_ref[...], (nb, SSM_SL_ST))
    ai = jnp.broadcast_to(abi_ref[...], (nb, SSM_SL_ST))

    def step(t, carry):
        hr, hi = carry
        rows = pl.ds(t, nb, stride=tc)
        in_r = jnp.concatenate([bur[c, rows, :] for c in range(nlb)], axis=1)
        in_i = jnp.concatenate([bui[c, rows, :] for c in range(nlb)], axis=1)
        nhr = ar * hr - ai * hi + in_r
        nhi = ar * hi + ai * hr + in_i
        for c in range(nlb):
            bur[c, rows, :] = nhr[:, c * LANE:(c + 1) * LANE]
            bui[c, rows, :] = nhi[:, c * LANE:(c + 1) * LANE]
        return nhr, nhi

    hr, hi = lax.fori_loop(0, tc, step, (hr_s[...], hi_s[...]))
    hr_s[...] = hr
    hi_s[...] = hi
    htr_ref[...] = hr
    hti_ref[...] = hi
    hr_all = jnp.concatenate([bur[c] for c in range(nlb)], axis=1)
    hi_all = jnp.concatenate([bui[c] for c in range(nlb)], axis=1)
    y = (_dot(hr_all.astype(BF16), cr_ref[...].astype(BF16))
         - _dot(hi_all.astype(BF16), ci_ref[...].astype(BF16)) + d_ref[...] * u)
    y_ref[...] = _gelu_tanh(y).reshape(nb, tc, SSM_SL_IN)


def ssm_prompt(proj3, u_col0, bbr, bbi, cr, ci, abr, abi, dsk, h0r, h0i, tc):
    nb, t, _ = proj3.shape
    cb0 = u_col0 // SSM_SL_IN
    sl = lambda shape: pl.BlockSpec((None,) + shape, lambda s, j: (s, 0, 0))
    kern = functools.partial(_ssm_prompt_kernel, nb=nb, tc=tc)
    return pl.pallas_call(
        kern,
        grid=(SSM_SLICES, t // tc),
        in_specs=[pl.BlockSpec((nb, tc, SSM_SL_IN), lambda s, j: (0, j, cb0 + s)),
                  sl((SSM_SL_IN, SSM_SL_ST)), sl((SSM_SL_IN, SSM_SL_ST)),
                  sl((SSM_SL_ST, SSM_SL_IN)), sl((SSM_SL_ST, SSM_SL_IN)),
                  sl((1, SSM_SL_ST)), sl((1, SSM_SL_ST)), sl((1, SSM_SL_IN)),
                  sl((nb, SSM_SL_ST)), sl((nb, SSM_SL_ST))],
        out_specs=(pl.BlockSpec((nb, tc, SSM_SL_IN), lambda s, j: (0, j, s)),
                   sl((nb, SSM_SL_ST)), sl((nb, SSM_SL_ST))),
        out_shape=(jax.ShapeDtypeStruct((nb, t, SSM_W), F32),
                   jax.ShapeDtypeStruct((SSM_SLICES, nb, SSM_SL_ST), F32),
                   jax.ShapeDtypeStruct((SSM_SLICES, nb, SSM_SL_ST), F32)),
        scratch_shapes=[pltpu.VMEM((SSM_SL_ST // LANE, nb * tc, LANE), F32),
                        pltpu.VMEM((SSM_SL_ST // LANE, nb * tc, LANE), F32),
                        pltpu.VMEM((nb, SSM_SL_ST), F32), pltpu.VMEM((nb, SSM_SL_ST), F32)],
        compiler_params=_cparams(("parallel", "arbitrary"), 48),
        name="ssm_prompt",
    )(proj3, bbr, bbi, cr, ci, abr, abi, dsk, h0r, h0i)


def _ssm_sample_kernel(u_ref, bbr_ref, bbi_ref, cr_ref, ci_ref, abr_ref, abi_ref, d_ref,
                       h0r_ref, h0i_ref, y_ref, htr_ref, hti_ref, *, nt):
    bbr_hi, bbr_lo = _split_bf16(bbr_ref[...])
    bbi_hi, bbi_lo = _split_bf16(bbi_ref[...])
    crb = cr_ref[...].astype(BF16)
    cib = ci_ref[...].astype(BF16)
    ar, ai = abr_ref[...], abi_ref[...]
    hr, hi = h0r_ref[...], h0i_ref[...]
    for t in range(nt):
        u = u_ref[t]
        u_hi, u_lo = _split_bf16(u)
        bu_r = _dot(u_hi, bbr_hi) + (_dot(u_hi, bbr_lo) + _dot(u_lo, bbr_hi))
        bu_i = _dot(u_hi, bbi_hi) + (_dot(u_hi, bbi_lo) + _dot(u_lo, bbi_hi))
        hr, hi = ar * hr - ai * hi + bu_r, ar * hi + ai * hr + bu_i
        y = _dot(hr.astype(BF16), crb) - _dot(hi.astype(BF16), cib) + d_ref[...] * u
        y_ref[t] = _gelu_tanh(y)
    htr_ref[...] = hr
    hti_ref[...] = hi


def ssm_sample(proj3, u_col0, bbr, bbi, cr, ci, abr, abi, dsk, h0r, h0i):
    nt, nb, _ = proj3.shape
    cb0 = u_col0 // SSM_SL_IN
    sl = lambda shape: pl.BlockSpec((None,) + shape, lambda s: (s, 0, 0))
    kern = functools.partial(_ssm_sample_kernel, nt=nt)
    return pl.pallas_call(
        kern,
        grid=(SSM_SLICES,),
        in_specs=[pl.BlockSpec((nt, nb, SSM_SL_IN), lambda s: (0, 0, cb0 + s)),
                  sl((SSM_SL_IN, SSM_SL_ST)), sl((SSM_SL_IN, SSM_SL_ST)),
                  sl((SSM_SL_ST, SSM_SL_IN)), sl((SSM_SL_ST, SSM_SL_IN)),
                  sl((1, SSM_SL_ST)), sl((1, SSM_SL_ST)), sl((1, SSM_SL_IN)),
                  sl((nb, SSM_SL_ST)), sl((nb, SSM_SL_ST))],
        out_specs=(pl.BlockSpec((nt, nb, SSM_SL_IN), lambda s: (0, 0, s)),
                   sl((nb, SSM_SL_ST)), sl((nb, SSM_SL_ST))),
        out_shape=(jax.ShapeDtypeStruct((nt, nb, SSM_W), F32),
                   jax.ShapeDtypeStruct((SSM_SLICES, nb, SSM_SL_ST), F32),
                   jax.ShapeDtypeStruct((SSM_SLICES, nb, SSM_SL_ST), F32)),
        compiler_params=_cparams(("parallel",), 32),
        name="ssm_sample",
    )(proj3, bbr, bbi, cr, ci, abr, abi, dsk, h0r, h0i)


def _conv_prompt_kernel(h_ref, cb_ref, cc_ref, w_ref, buf_ref, cv_ref, nbuf_ref, cat, *, t):
    pad = 8
    cat[pad - 2:pad, :] = buf_ref[...]
    cat[pad:pad + t, :] = cc_ref[...] * h_ref[...]
    w = w_ref[...]
    y = (w[0:1] * cat[pad - 2:pad - 2 + t, :] + w[1:2] * cat[pad - 1:pad - 1 + t, :]
         + w[2:3] * cat[pad:pad + t, :])
    cv_ref[...] = cb_ref[...] * y
    nbuf_ref[...] = cat[pad + t - 2:pad + t, :]


def conv_prompt(proj3, h_col0, conv_w, buf):
    nb, t, _ = proj3.shape
    c0 = h_col0 // LANE
    nblk = CONV_W // LANE
    kern = functools.partial(_conv_prompt_kernel, t=t)
    col = lambda off: pl.BlockSpec((None, t, LANE), lambda b, c: (b, 0, c0 + off + c))
    return pl.pallas_call(
        kern,
        grid=(nb, nblk),
        in_specs=[col(0), col(nblk), col(2 * nblk),
                  pl.BlockSpec((CONV_TAPS, LANE), lambda b, c: (0, c)),
                  pl.BlockSpec((None, CONV_TAPS - 1, LANE), lambda b, c: (b, 0, c))],
        out_specs=(pl.BlockSpec((None, t, LANE), lambda b, c: (b, 0, c)),
                   pl.BlockSpec((None, CONV_TAPS - 1, LANE), lambda b, c: (b, 0, c))),
        out_shape=(jax.ShapeDtypeStruct((nb, t, CONV_W), F32),
                   jax.ShapeDtypeStruct((nb, CONV_TAPS - 1, CONV_W), F32)),
        scratch_shapes=[pltpu.VMEM((t + 8, LANE), F32)],
        compiler_params=_cparams(("parallel", "parallel"), 32),
        name="conv_prompt",
    )(proj3, proj3, proj3, conv_w, buf)


def _conv_sample_kernel(h_ref, cb_ref, cc_ref, w_ref, buf_ref, cv_ref, nbuf_ref, *, nt):
    w = w_ref[...]
    rows = [buf_ref[0], buf_ref[1]] + [cc_ref[t] * h_ref[t] for t in range(nt)]
    for t in range(nt):
        y = w[0:1] * rows[t] + w[1:2] * rows[t + 1] + w[2:3] * rows[t + 2]
        cv_ref[t] = cb_ref[t] * y
    nbuf_ref[0] = rows[nt]
    nbuf_ref[1] = rows[nt + 1]


def conv_sample(proj3, h_col0, conv_w, buf_t):
    nt, nb, _ = proj3.shape
    c0 = h_col0 // CONV_W
    kern = functools.partial(_conv_sample_kernel, nt=nt)
    col = lambda off: pl.BlockSpec((nt, nb, CONV_W), lambda i: (0, 0, c0 + off))
    return pl.pallas_call(
        kern,
        grid=(1,),
        in_specs=[col(0), col(1), col(2),
                  pl.BlockSpec((CONV_TAPS, CONV_W), lambda i: (0, 0)),
                  pl.BlockSpec((CONV_TAPS - 1, nb, CONV_W), lambda i: (0, 0, 0))],
        out_specs=(pl.BlockSpec((nt, nb, CONV_W), lambda i: (0, 0, 0)),
                   pl.BlockSpec((CONV_TAPS - 1, nb, CONV_W), lambda i: (0, 0, 0))),
        out_shape=(jax.ShapeDtypeStruct((nt, nb, CONV_W), F32),
                   jax.ShapeDtypeStruct((CONV_TAPS - 1, nb, CONV_W), F32)),
        compiler_params=_cparams(("arbitrary",), 32),
        name="conv_sample",
    )(proj3, proj3, proj3, conv_w, buf_t)


def _suffix_matrices(n):
    r = jnp.arange(n)[:, None]
    c = jnp.arange(n + LANE)[None, :]
    return jnp.where((r > c) | (c >= n), 1.0, 0.0).astype(BF16)


def _att_prompt_kernel(bias_ref, q_ref, k_ref, v_ref, maug_ref, mtail_ref, o_ref, kbf, vbf, acc, car, *, t):
    npair = N_HEADS // 2
    nfull = t // KBLK
    tail = t - nfull * KBLK
    kbf[...] = k_ref[...].astype(BF16)
    vbf[...] = v_ref[...].astype(BF16)
    pairs = range(npair)
    cols = [slice(p * LANE, (p + 1) * LANE) for p in pairs]

    def qblock(row0, nq, n_prev):
        lane = lax.broadcasted_iota(jnp.int32, (nq, LANE), 1)
        row = lax.broadcasted_iota(jnp.int32, (2 * nq, 1), 0)
        q2, bias2 = [], []
        for p in pairs:
            q = q_ref[pl.ds(row0, nq), cols[p]] * QK_SCALE
            q2.append(jnp.concatenate([jnp.where(lane < HEAD_DIM, q, 0.0),
                                       jnp.where(lane >= HEAD_DIM, q, 0.0)], axis=0).astype(BF16))
            bias2.append(jnp.where(row < nq, bias_ref[2 * p], bias_ref[2 * p + 1]))
        ri = lax.broadcasted_iota(jnp.int32, (2 * nq, nq), 0)
        ci = lax.broadcasted_iota(jnp.int32, (2 * nq, nq), 1)
        vis = ci < jnp.where(ri >= nq, ri - nq, ri)
        mref = maug_ref if nq == KBLK else mtail_ref
        zs = [_dot_nt(q2[p], kbf[pl.ds(row0, nq), cols[p]]) + bias2[p] for p in pairs]
        sps = [_softplus(z) for z in zs]
        hilo = [_split_bf16(jnp.where(vis, -sp, 0.0)) for sp in sps]
        s_augs = [_dot(hi, mref[...]) + _dot(lo, mref[...]) for hi, lo in hilo]
        ws = [jnp.where(vis, jnp.exp(zs[p] - sps[p] + s_augs[p][:, :nq]), 0.0).astype(BF16) for p in pairs]
        for p in pairs:
            acc[p, 0:2 * nq, :] = _dot(ws[p], vbf[pl.ds(row0, nq), cols[p]])
            car[p, 0:2 * nq, :] = s_augs[p][:, nq:]

        def step(r, nk):
            nj = nk // KBLK
            zs = [_dot_nt(q2[p], kbf[pl.ds(r, nk), cols[p]]) + bias2[p] for p in pairs]
            sps = [_softplus(z) for z in zs]
            m2 = jnp.concatenate([maug_ref[...], maug_ref[...]], axis=0)
            s_parts = []
            for sp in sps:
                hi, lo = _split_bf16(-sp)
                s_parts.append([_dot(jnp.concatenate([hi[:, j * KBLK:(j + 1) * KBLK],
                                                      lo[:, j * KBLK:(j + 1) * KBLK]], axis=1), m2)
                                for j in range(nj)])
            ws, new_car = [], []
            for p in pairs:
                add = car[p, 0:2 * nq, :]
                sufs = [None] * nj
                for j in reversed(range(nj)):
                    sufs[j] = s_parts[p][j][:, :KBLK] + add
                    add = add + s_parts[p][j][:, KBLK:]
                suf = sufs[0] if nj == 1 else jnp.concatenate(sufs, axis=1)
                ws.append(jnp.exp(zs[p] - sps[p] + suf).astype(BF16))
                new_car.append(add)
            for p in pairs:
                acc[p, 0:2 * nq, :] += _dot(ws[p], vbf[pl.ds(r, nk), cols[p]])
                car[p, 0:2 * nq, :] = new_car[p]

        def body(i, carry):
            step(pl.multiple_of((n_prev - 2 - 2 * i) * KBLK, KBLK), 2 * KBLK)
            return carry

        lax.fori_loop(0, n_prev // 2, body, 0)
        if isinstance(n_prev, int):
            if n_prev % 2:
                step(0, KBLK)
        else:
            @pl.when(n_prev % 2 == 1)
            def _():
                step(0, KBLK)
        for p in pairs:
            o2 = acc[p, 0:2 * nq, :]
            o_ref[pl.ds(row0, nq), cols[p]] = jnp.where(lane < HEAD_DIM, o2[0:nq], o2[nq:2 * nq])

    def outer(i, carry):
        qblock(pl.multiple_of(i * KBLK, KBLK), KBLK, i)
        return carry

    lax.fori_loop(0, nfull, outer, 0)
    if tail:
        qblock(nfull * KBLK, tail, nfull)


def attention_prompt(proj3, bias):
    nb, t, _ = proj3.shape
    tail = t % KBLK
    npair = N_HEADS // 2
    col = lambda i: pl.BlockSpec((None, t, ATT_WIDTH), lambda b: (b, 0, i))
    maug = _suffix_matrices(KBLK)
    mtail = _suffix_matrices(tail if tail else 8)
    kern = functools.partial(_att_prompt_kernel, t=t)
    return pl.pallas_call(
        kern,
        grid=(nb,),
        in_specs=[pl.BlockSpec(memory_space=pltpu.SMEM),
                  col(0), col(1), col(2),
                  pl.BlockSpec(maug.shape, lambda b: (0, 0)),
                  pl.BlockSpec(mtail.shape, lambda b: (0, 0))],
        out_specs=pl.BlockSpec((None, t, ATT_WIDTH), lambda b: (b, 0, 0)),
        out_shape=jax.ShapeDtypeStruct((nb, t, ATT_WIDTH), F32),
        scratch_shapes=[pltpu.VMEM((t, ATT_WIDTH), BF16), pltpu.VMEM((t, ATT_WIDTH), BF16),
                        pltpu.VMEM((npair, 2 * KBLK, LANE), F32), pltpu.VMEM((npair, 2 * KBLK, LANE), F32)],
        compiler_params=_cparams(("parallel",), 52),
        name="attention_prompt",
    )(bias, proj3, proj3, proj3, maug, mtail)


def _att_paged_kernel(pt_ref, q_ref, kn_ref, vn_ref, bias_ref, maug_ref, *refs, npages, ns):
    del pt_ref
    k_refs = refs[:npages]
    v_refs = refs[npages:2 * npages]
    o_ref = refs[2 * npages]
    nr = ns * N_HEADS
    q = q_ref[...] * QK_SCALE
    lane = lax.broadcasted_iota(jnp.int32, (N_HEADS, ATT_WIDTH), 1)
    hrow = lax.broadcasted_iota(jnp.int32, (N_HEADS, ATT_WIDTH), 0)
    hmask = (lane >= hrow * HEAD_DIM) & (lane < (hrow + 1) * HEAD_DIM)
    wq = jnp.concatenate(
        [jnp.where(hmask, jnp.broadcast_to(q[i:i + 1, :], (N_HEADS, ATT_WIDTH)), 0.0) for i in range(ns)],
        axis=0)
    wqb = wq.astype(BF16)
    bias = jnp.concatenate([bias_ref[...]] * ns, axis=0)
    bias1 = bias[:, 0:1]
    qidx = lax.broadcasted_iota(jnp.int32, (nr, 1), 0) // N_HEADS

    kn = kn_ref[...]
    vn = vn_ref[...]
    zs, sps, lks, vis = [], [], [], []
    for j in range(ns):
        zj = jnp.sum(wq * kn[j:j + 1, :], axis=-1, keepdims=True) + bias1
        vj = qidx > j
        spj = _softplus(zj)
        zs.append(zj); sps.append(spj); vis.append(vj)
        lks.append(jnp.where(vj, -spj, 0.0))
    o = jnp.zeros((nr, ATT_WIDTH), F32)
    suf = jnp.zeros((nr, 1), F32)
    for j in reversed(range(ns)):
        wj = jnp.where(vis[j], jnp.exp(zs[j] - sps[j] + suf), 0.0)
        o = o + wj * vn[j:j + 1, :]
        suf = suf + lks[j]
    car = jnp.broadcast_to(suf, (nr, LANE))

    pages = range(npages)
    zs = [_dot(wqb, k_refs[p][...].astype(BF16)) + bias for p in pages]
    sps = [_softplus(z) for z in zs]
    hilo = [_split_bf16(-sp) for sp in sps]
    hi_st = jnp.concatenate([h for h, _ in hilo], axis=0)
    lo_st = jnp.concatenate([l for _, l in hilo], axis=0)
    s_st = _dot(hi_st, maug_ref[...]) + _dot(lo_st, maug_ref[...])
    cars = [None] * npages
    for p in reversed(pages):
        cars[p] = car
        car = car + s_st[p * nr:(p + 1) * nr, KBLK:]
    for p in pages:
        w = jnp.exp(zs[p] - sps[p] + s_st[p * nr:(p + 1) * nr, :KBLK] + cars[p])
        o = o + _dot_nt(w.astype(BF16), v_refs[p][...].astype(BF16))

    o3 = o.reshape(ns, N_HEADS, ATT_WIDTH)
    o_ref[...] = jnp.sum(jnp.where(hmask[None], o3, 0.0), axis=1)


def attention_paged(q, kn, vn, bias, cache_k, cache_v, layer, page_table):
    nb, ns, _ = q.shape
    npages = page_table.shape[1]
    maug = _suffix_matrices(KBLK)
    bias_rep = jnp.broadcast_to(bias[:, None], (N_HEADS, LANE))
    row = pl.BlockSpec((None, ns, ATT_WIDTH), lambda b, pt: (b, 0, 0))

    def page_spec(p):
        return pl.BlockSpec((None, None, ATT_WIDTH, PAGE), lambda b, pt: (layer, pt[b, p], 0, 0))

    kern = functools.partial(_att_paged_kernel, npages=npages, ns=ns)
    return pl.pallas_call(
        kern,
        grid_spec=pltpu.PrefetchScalarGridSpec(
            num_scalar_prefetch=1,
            grid=(nb,),
            in_specs=[row, row, row,
                      pl.BlockSpec((N_HEADS, LANE), lambda b, pt: (0, 0)),
                      pl.BlockSpec(maug.shape, lambda b, pt: (0, 0))]
                     + [page_spec(p) for p in range(npages)] * 2,
            out_specs=row),
        out_shape=jax.ShapeDtypeStruct((nb, ns, ATT_WIDTH), F32),
        compiler_params=_cparams(("arbitrary",), 40),
        name="attention_paged",
    )(page_table, q, kn, vn, bias_rep, maug, *([cache_k] * npages), *([cache_v] * npages))


def _mix_kernel(x_ref, att_ref, ssm_ref, cv_ref, gate_ref, wa_ref, wg_ref, wc_ref, wo_ref,
                g_ref, b_ref, o_ref, *, d):
    y_att = _dot(att_ref[...].astype(BF16), wa_ref[...])
    zg = _dot(ssm_ref[...].astype(BF16), wg_ref[...])
    y_ssm = zg[:, :d] * _sigmoid(zg[:, d:])
    y_conv = _dot(cv_ref[...].astype(BF16), wc_ref[...])
    mix = (_sigmoid(gate_ref[:, 0:d]) * y_att + _sigmoid(gate_ref[:, d:2 * d]) * y_ssm
           + _sigmoid(gate_ref[:, 2 * d:3 * d]) * y_conv)
    r = ALPHA * x_ref[...] + _dot(mix.astype(BF16), wo_ref[...])
    o_ref[...] = _layer_norm(r, g_ref[...], b_ref[...])


def mix_block(x, att, ssm, cv, gates, wa, wg, wc, wo, g, b, tm):
    n, d = x.shape
    rows = lambda w: pl.BlockSpec((tm, w), lambda i: (i, 0))
    full = lambda a: pl.BlockSpec(a.shape, lambda i: (0, 0))
    g2, b2 = g.reshape(1, d), b.reshape(1, d)
    kern = functools.partial(_mix_kernel, d=d)
    return pl.pallas_call(
        kern,
        grid=(n // tm,),
        in_specs=[rows(d), rows(att.shape[1]), rows(ssm.shape[1]), rows(cv.shape[1]), rows(gates.shape[1]),
                  full(wa), full(wg), full(wc), full(wo), full(g2), full(b2)],
        out_specs=rows(d),
        out_shape=jax.ShapeDtypeStruct((n, d), F32),
        compiler_params=_cparams(("parallel",), 52),
        name="mix_block",
    )(x, att, ssm, cv, gates, wa, wg, wc, wo, g2, b2)


def _mlp_kernel(x_ref, wu_ref, wd_ref, g_ref, b_ref, o_ref, acc_ref, xb_ref):
    c = pl.program_id(1)

    @pl.when(c == 0)
    def _():
        acc_ref[...] = jnp.zeros_like(acc_ref)
        xb_ref[...] = x_ref[...].astype(BF16)

    h = jnp.maximum(_dot(xb_ref[...], wu_ref[...]), 0.0)
    acc_ref[...] += _dot((h * h).astype(BF16), wd_ref[...])

    @pl.when(c == pl.num_programs(1) - 1)
    def _():
        o_ref[...] = _layer_norm(ALPHA * x_ref[...] + acc_ref[...], g_ref[...], b_ref[...])


def mlp_block(x, wu, wd, g, b, tm, tf):
    n, d = x.shape
    f = wu.shape[1]
    return pl.pallas_call(
        _mlp_kernel,
        grid=(n // tm, f // tf),
        in_specs=[pl.BlockSpec((tm, d), lambda i, c: (i, 0)),
                  pl.BlockSpec((d, tf), lambda i, c: (0, c)),
                  pl.BlockSpec((tf, d), lambda i, c: (c, 0)),
                  pl.BlockSpec((1, d), lambda i, c: (0, 0)),
                  pl.BlockSpec((1, d), lambda i, c: (0, 0))],
        out_specs=pl.BlockSpec((tm, d), lambda i, c: (i, 0)),
        out_shape=jax.ShapeDtypeStruct((n, d), F32),
        scratch_shapes=[pltpu.VMEM((tm, d), F32), pltpu.VMEM((tm, d), BF16)],
        compiler_params=_cparams(("parallel", "arbitrary"), 48),
        name="mlp_block",
    )(x, wu, wd, g.reshape(1, d), b.reshape(1, d))


def _row_tile(n, target):
    best = 8
    for c in range(8, target + 1, 8):
        if n % c == 0:
            best = c
    return best


def _slice_state(h):
    nb = h.shape[0]
    return h.reshape(nb, SSM_SLICES, SSM_SL_ST).transpose(1, 0, 2)


def _unslice_state(h):
    nb = h.shape[1]
    return h.transpose(1, 0, 2).reshape(nb, SSM_NGRP, SSM_NSTATE)


def kernel(x_prompt, x_sample, cache_k, cache_v, state_ssm_re, state_ssm_im, state_conv, page_table, meta_tokens, ln_in_g, ln_in_b, w_in, sb_bias, ssm_lam_re, ssm_lam_im, ssm_log_dt, ssm_b_re, ssm_b_im, ssm_c_re, ssm_c_im, ssm_d, w_glu, conv_w, w_branch_a, w_branch_c, w_out, ln1_g, ln1_b, w_up, w_down, ln2_g, ln2_b):
    nb, seq, d = x_prompt.shape
    sb, ss, _ = x_sample.shape
    t = seq + N_META_TOK
    n_p, n_s = nb * t, sb * ss
    depth = w_in.shape[0]

    w_main = w_in[:, :, :MAIN_W].astype(BF16)
    w_gate = w_in[:, :, MAIN_W:].astype(BF16)
    wa, wg, wc, wo = (w.astype(BF16) for w in (w_branch_a, w_glu, w_branch_c, w_out))
    wu, wd = w_up.astype(BF16), w_down.astype(BF16)

    abr, abi, bbr_t, bbi_t = ssm_prep(ssm_lam_re, ssm_lam_im, ssm_log_dt, ssm_b_re, ssm_b_im)
    abr = abr.reshape(depth, SSM_SLICES, 1, SSM_SL_ST)
    abi = abi.reshape(depth, SSM_SLICES, 1, SSM_SL_ST)
    bbr_t = bbr_t.reshape(depth, SSM_NGRP, SSM_GRP, SSM_NSTATE)
    bbi_t = bbi_t.reshape(depth, SSM_NGRP, SSM_GRP, SSM_NSTATE)
    dsk = ssm_d.reshape(depth, SSM_SLICES, 1, SSM_SL_IN)
    cache_k4 = cache_k.transpose(0, 1, 3, 4, 2).reshape(cache_k.shape[0], cache_k.shape[1], ATT_WIDTH, PAGE)
    cache_v4 = cache_v.transpose(0, 1, 3, 4, 2).reshape(cache_v.shape[0], cache_v.shape[1], ATT_WIDTH, PAGE)

    tm_p = _row_tile(n_p, 688)
    tm_mix = _row_tile(n_p, 344)
    tm_s = _row_tile(n_s, 256)
    tc = _row_tile(t, 344)

    meta = jnp.broadcast_to(meta_tokens.astype(x_prompt.dtype)[None], (nb, N_META_TOK, d))
    xp = layernorm_rows(jnp.concatenate([meta, x_prompt], axis=1).reshape(n_p, d), ln_in_g, ln_in_b, tm_p)
    xs = layernorm_rows(x_sample.transpose(1, 0, 2).reshape(n_s, d), ln_in_g, ln_in_b, tm_s)

    h0_zero = jnp.zeros((SSM_SLICES, nb, SSM_SL_ST), F32)
    buf_zero = jnp.zeros((nb, CONV_TAPS - 1, CONV_W), F32)
    outs = [[] for _ in range(10)]
    u_col, h_col = 3 * ATT_WIDTH, 3 * ATT_WIDTH + SSM_W
    for l in range(depth):
        bbr = _block_diag_slices(bbr_t[l])
        bbi = _block_diag_slices(bbi_t[l])
        cr = _block_diag_slices(jnp.swapaxes(ssm_c_re[l], -1, -2))
        ci = _block_diag_slices(jnp.swapaxes(ssm_c_im[l], -1, -2))
        ssm_w = (bbr, bbi, cr, ci, abr[l], abi[l], dsk[l])

        main = matmul_xw(xp, w_main[l], tm_p, MAIN_W // 4).reshape(nb, t, MAIN_W)
        gates = matmul_xw(xp, w_gate[l], tm_p, (3 * d) // 4)
        att = attention_prompt(main, sb_bias[l])
        ysm, htr, hti = ssm_prompt(main, u_col, *ssm_w, h0_zero, h0_zero, tc)
        cv, nbuf = conv_prompt(main, h_col, conv_w[l], buf_zero)
        x1 = mix_block(xp, att.reshape(n_p, -1), ysm.reshape(n_p, -1), cv.reshape(n_p, -1), gates,
                       wa[l], wg[l], wc[l], wo[l], ln1_g[l], ln1_b[l], tm_mix)
        xp = mlp_block(x1, wu[l], wd[l], ln2_g[l], ln2_b[l], tm_p, 1024)
        outs[0].append(main[:, :, ATT_WIDTH:2 * ATT_WIDTH].reshape(nb, t, N_HEADS, HEAD_DIM))
        outs[1].append(main[:, :, 2 * ATT_WIDTH:3 * ATT_WIDTH].reshape(nb, t, N_HEADS, HEAD_DIM))
        outs[4].append(_unslice_state(htr))
        outs[5].append(_unslice_state(hti))
        outs[8].append(nbuf)

        main_s = matmul_xw(xs, w_main[l], tm_s, MAIN_W // 4).reshape(ss, sb, MAIN_W)
        gates_s = matmul_xw(xs, w_gate[l], tm_s, (3 * d) // 4)
        qkv = main_s[:, :, :3 * ATT_WIDTH].transpose(1, 0, 2)
        q_s, k_s, v_s = (qkv[:, :, i * ATT_WIDTH:(i + 1) * ATT_WIDTH] for i in range(3))
        att_s = attention_paged(q_s, k_s, v_s, sb_bias[l], cache_k4, cache_v4, l, page_table)
        att_s = att_s.transpose(1, 0, 2).reshape(n_s, ATT_WIDTH)
        ysm_s, htr_s, hti_s = ssm_sample(main_s, u_col, *ssm_w,
                                         _slice_state(state_ssm_re[l]), _slice_state(state_ssm_im[l]))
        cv_s, nbuf_s = conv_sample(main_s, h_col, conv_w[l], state_conv[l].transpose(1, 0, 2))
        x1_s = mix_block(xs, att_s, ysm_s.reshape(n_s, -1), cv_s.reshape(n_s, -1), gates_s,
                         wa[l], wg[l], wc[l], wo[l], ln1_g[l], ln1_b[l], tm_s)
        xs = mlp_block(x1_s, wu[l], wd[l], ln2_g[l], ln2_b[l], tm_s, 1024)
        outs[2].append(k_s.reshape(sb, ss, N_HEADS, HEAD_DIM))
        outs[3].append(v_s.reshape(sb, ss, N_HEADS, HEAD_DIM))
        outs[6].append(_unslice_state(htr_s))
        outs[7].append(_unslice_state(hti_s))
        outs[9].append(nbuf_s.transpose(1, 0, 2))

    y_prompt = xp.reshape(nb, t, d)[:, N_META_TOK:]
    y_sample = xs.reshape(ss, sb, d).transpose(1, 0, 2)
    return (y_prompt, y_sample) + tuple(jnp.stack(o) for o in outs)
```

```python
import functools
import math

import jax
import jax.numpy as jnp
from jax import lax
from jax.experimental import pallas as pl
from jax.experimental.pallas import tpu as pltpu

F32 = jnp.float32
BF16 = jnp.bfloat16

N_LAYERS = 4
N_META_TOK = 16
N_HEADS = 8
HEAD_DIM = 64
ATT_WIDTH = N_HEADS * HEAD_DIM
SSM_W = 512
SSM_GRP = 16
SSM_NGRP = SSM_W // SSM_GRP
SSM_NSTATE = 64
CONV_W = 512
CONV_TAPS = 3
PAGE = 128
ALPHA = (2 * N_LAYERS) ** 0.25
EPS = 1e-5
QK_SCALE = HEAD_DIM ** -0.5

MAIN_W = 3 * ATT_WIDTH + SSM_W + 3 * CONV_W
LANE = 128
KBLK = 128
QBLK = 256
SSM_SLICES = 4
SSM_SL_IN = SSM_W // SSM_SLICES
SSM_SL_ST = SSM_NGRP * SSM_NSTATE // SSM_SLICES
VMEM_CAP = 56 * 1024 * 1024


def _cparams(sem, vmem_mb):
    return pltpu.CompilerParams(dimension_semantics=sem,
                                vmem_limit_bytes=min(vmem_mb * 1024 * 1024, VMEM_CAP))


def _layer_norm(xf, g, b):
    mu = jnp.mean(xf, axis=-1, keepdims=True)
    xc = xf - mu
    var = jnp.mean(xc * xc, axis=-1, keepdims=True)
    return xc * lax.rsqrt(var + EPS) * g + b


def _sigmoid(x):
    return 1.0 / (1.0 + jnp.exp(-x))


def _gelu_tanh(x):
    c = math.sqrt(2.0 / math.pi)
    return 0.5 * x * (1.0 + jnp.tanh(c * (x + 0.044715 * (x * x * x))))


def _softplus(z):
    return jnp.maximum(z, 0.0) + jnp.log(1.0 + jnp.exp(-jnp.abs(z)))


def _dot(a, b):
    return jnp.dot(a, b, preferred_element_type=F32)


def _dot_nt(a, b):
    return lax.dot_general(a, b, (((1,), (1,)), ((), ())), preferred_element_type=F32)


def _split_bf16(x):
    hi = x.astype(BF16)
    lo = (x - hi.astype(F32)).astype(BF16)
    return hi, lo


def _ln_kernel(x_ref, g_ref, b_ref, o_ref):
    o_ref[...] = _layer_norm(x_ref[...], g_ref[...], b_ref[...])


def layernorm_rows(x, g, b, tm):
    n, d = x.shape
    return pl.pallas_call(
        _ln_kernel,
        grid=(n // tm,),
        in_specs=[pl.BlockSpec((tm, d), lambda i: (i, 0)),
                  pl.BlockSpec((1, d), lambda i: (0, 0)),
                  pl.BlockSpec((1, d), lambda i: (0, 0))],
        out_specs=pl.BlockSpec((tm, d), lambda i: (i, 0)),
        out_shape=jax.ShapeDtypeStruct((n, d), F32),
        compiler_params=_cparams(("parallel",), 32),
        name="layernorm_rows",
    )(x, g.reshape(1, d), b.reshape(1, d))


def _mm_kernel(x_ref, w_ref, o_ref):
    o_ref[...] = _dot(x_ref[...].astype(BF16), w_ref[...])


def matmul_xw(x, w_all, layer, tm, tn):
    m, k = x.shape
    n = w_all.shape[2]
    return pl.pallas_call(
        _mm_kernel,
        grid=(n // tn, m // tm),
        in_specs=[pl.BlockSpec((tm, k), lambda j, i: (i, 0)),
                  pl.BlockSpec((None, k, tn), lambda j, i: (layer, 0, j))],
        out_specs=pl.BlockSpec((tm, tn), lambda j, i: (i, j)),
        out_shape=jax.ShapeDtypeStruct((m, n), F32),
        compiler_params=_cparams(("parallel", "parallel"), 40),
        name="matmul_xw",
    )(x, w_all)


def _ssm_prep_kernel(lr_ref, li_ref, ldt_ref, br_ref, bi_ref, abr_ref, abi_ref, bbr_ref, bbi_ref):
    lr, li = lr_ref[...], li_ref[...]
    dt = jnp.exp(ldt_ref[...])
    mag = jnp.exp(lr * dt)
    ab_re = mag * jnp.cos(li * dt)
    ab_im = mag * jnp.sin(li * dt)
    den = lr * lr + li * li
    nr, ni = ab_re - 1.0, ab_im
    fr = (nr * lr + ni * li) / den
    fi = (ni * lr - nr * li) / den
    abr_ref[...] = ab_re
    abi_ref[...] = ab_im
    br, bi = br_ref[...], bi_ref[...]
    bbr_ref[...] = fr * br - fi * bi
    bbi_ref[...] = fr * bi + fi * br


def ssm_prep(lam_re, lam_im, log_dt, b_re, b_im):
    lg = lam_re.shape[0] * lam_re.shape[1]
    n, p = SSM_NSTATE, SSM_GRP
    args = (lam_re.reshape(lg, 1, n), lam_im.reshape(lg, 1, n), log_dt.reshape(lg, 1, 1),
            jnp.swapaxes(b_re, -1, -2).reshape(lg, p, n), jnp.swapaxes(b_im, -1, -2).reshape(lg, p, n))
    full = lambda a: pl.BlockSpec(a.shape, lambda i: (0,) * a.ndim)
    outs = (jax.ShapeDtypeStruct((lg, 1, n), F32),) * 2 + (jax.ShapeDtypeStruct((lg, p, n), F32),) * 2
    return pl.pallas_call(
        _ssm_prep_kernel,
        grid=(1,),
        in_specs=[full(a) for a in args],
        out_specs=tuple(pl.BlockSpec(o.shape, lambda i: (0, 0, 0)) for o in outs),
        out_shape=outs,
        compiler_params=_cparams(("arbitrary",), 32),
        name="ssm_prep",
    )(*args)


def _block_diag_slices(m):
    g, r, c = m.shape
    per = g // SSM_SLICES
    m4 = m.reshape(SSM_SLICES, per, r, c)
    eye = jnp.eye(per, dtype=m.dtype)
    return jnp.einsum('sjrc,jk->sjrkc', m4, eye).reshape(SSM_SLICES, per * r, per * c)


def _ssm_prompt_kernel(u_ref, bbr_ref, bbi_ref, cr_ref, ci_ref, abr_ref, abi_ref, d_ref,
                       h0r_ref, h0i_ref, y_ref, htr_ref, hti_ref, bur, bui, hr_s, hi_s, *, nb, tc):
    j = pl.program_id(1)

    @pl.when(j == 0)
    def _():
        hr_s[...] = h0r_ref[...]
        hi_s[...] = h0i_ref[...]

    u = u_ref[...].reshape(nb * tc, SSM_SL_IN)
    ub = u.astype(BF16)
    nlb = SSM_SL_ST // LANE
    bu_r = _dot(ub, bbr_ref[...].astype(BF16))
    bu_i = _dot(ub, bbi_ref[...].astype(BF16))
    for c in range(nlb):
        bur[c] = bu_r[:, c * LANE:(c + 1) * LANE]
        bui[c] = bu_i[:, c * LANE:(c + 1) * LANE]
    ar = jnp.broadcast_to(abr_ref[...], (nb, SSM_SL_ST))
    ai = jnp.broadcast_to(abi_ref[...], (nb, SSM_SL_ST))

    def step(t, carry):
        hr, hi = carry
        rows = pl.ds(t, nb, stride=tc)
        in_r = jnp.concatenate([bur[c, rows, :] for c in range(nlb)], axis=1)
        in_i = jnp.concatenate([bui[c, rows, :] for c in range(nlb)], axis=1)
        nhr = ar * hr - ai * hi + in_r
        nhi = ar * hi + ai * hr + in_i
        for c in range(nlb):
            bur[c, rows, :] = nhr[:, c * LANE:(c + 1) * LANE]
            bui[c, rows, :] = nhi[:, c * LANE:(c + 1) * LANE]
        return nhr, nhi

    hr, hi = lax.fori_loop(0, tc, step, (hr_s[...], hi_s[...]))
    hr_s[...] = hr
    hi_s[...] = hi
    htr_ref[...] = hr
    hti_ref[...] = hi
    hr_all = jnp.concatenate([bur[c] for c in range(nlb)], axis=1)
    hi_all = jnp.concatenate([bui[c] for c in range(nlb)], axis=1)
    y = (_dot(hr_all.astype(BF16), cr_ref[...].astype(BF16))
         - _dot(hi_all.astype(BF16), ci_ref[...].astype(BF16)) + d_ref[...] * u)
    y_ref[...] = _gelu_tanh(y).reshape(nb, tc, SSM_SL_IN)


def ssm_prompt(proj3, u_col0, bbr, bbi, cr, ci, abr, abi, dsk, h0r, h0i, tc):
    nb, t, _ = proj3.shape
    cb0 = u_col0 // SSM_SL_IN
    sl = lambda shape: pl.BlockSpec((None,) + shape, lambda s, j: (s, 0, 0))
    kern = functools.partial(_ssm_prompt_kernel, nb=nb, tc=tc)
    return pl.pallas_call(
        kern,
        grid=(SSM_SLICES, t // tc),
        in_specs=[pl.BlockSpec((nb, tc, SSM_SL_IN), lambda s, j: (0, j, cb0 + s)),
                  sl((SSM_SL_IN, SSM_SL_ST)), sl((SSM_SL_IN, SSM_SL_ST)),
                  sl((SSM_SL_ST, SSM_SL_IN)), sl((SSM_SL_ST, SSM_SL_IN)),
                  sl((1, SSM_SL_ST)), sl((1, SSM_SL_ST)), sl((1, SSM_SL_IN)),
                  sl((nb, SSM_SL_ST)), sl((nb, SSM_SL_ST))],
        out_specs=(pl.BlockSpec((nb, tc, SSM_SL_IN), lambda s, j: (0, j, s)),
                   sl((nb, SSM_SL_ST)), sl((nb, SSM_SL_ST))),
        out_shape=(jax.ShapeDtypeStruct((nb, t, SSM_W), F32),
                   jax.ShapeDtypeStruct((SSM_SLICES, nb, SSM_SL_ST), F32),
                   jax.ShapeDtypeStruct((SSM_SLICES, nb, SSM_SL_ST), F32)),
        scratch_shapes=[pltpu.VMEM((SSM_SL_ST // LANE, nb * tc, LANE), F32),
                        pltpu.VMEM((SSM_SL_ST // LANE, nb * tc, LANE), F32),
                        pltpu.VMEM((nb, SSM_SL_ST), F32), pltpu.VMEM((nb, SSM_SL_ST), F32)],
        compiler_params=_cparams(("parallel", "arbitrary"), 48),
        name="ssm_prompt",
    )(proj3, bbr, bbi, cr, ci, abr, abi, dsk, h0r, h0i)


def _ssm_sample_kernel(u_ref, bbr_ref, bbi_ref, cr_ref, ci_ref, abr_ref, abi_ref, d_ref,
                       h0r_ref, h0i_ref, y_ref, htr_ref, hti_ref, *, nt):
    bbr_hi, bbr_lo = _split_bf16(bbr_ref[...])
    bbi_hi, bbi_lo = _split_bf16(bbi_ref[...])
    crb = cr_ref[...].astype(BF16)
    cib = ci_ref[...].astype(BF16)
    ar, ai = abr_ref[...], abi_ref[...]
    hr, hi = h0r_ref[...], h0i_ref[...]
    for t in range(nt):
        u = u_ref[t]
        u_hi, u_lo = _split_bf16(u)
        bu_r = _dot(u_hi, bbr_hi) + (_dot(u_hi, bbr_lo) + _dot(u_lo, bbr_hi))
        bu_i = _dot(u_hi, bbi_hi) + (_dot(u_hi, bbi_lo) + _dot(u_lo, bbi_hi))
        hr, hi = ar * hr - ai * hi + bu_r, ar * hi + ai * hr + bu_i
        y = _dot(hr.astype(BF16), crb) - _dot(hi.astype(BF16), cib) + d_ref[...] * u
        y_ref[t] = _gelu_tanh(y)
    htr_ref[...] = hr
    hti_ref[...] = hi


def ssm_sample(proj3, u_col0, bbr, bbi, cr, ci, abr, abi, dsk, h0r, h0i):
    nt, nb, _ = proj3.shape
    cb0 = u_col0 // SSM_SL_IN
    sl = lambda shape: pl.BlockSpec((None,) + shape, lambda s: (s, 0, 0))
    kern = functools.partial(_ssm_sample_kernel, nt=nt)
    return pl.pallas_call(
        kern,
        grid=(SSM_SLICES,),
        in_specs=[pl.BlockSpec((nt, nb, SSM_SL_IN), lambda s: (0, 0, cb0 + s)),
                  sl((SSM_SL_IN, SSM_SL_ST)), sl((SSM_SL_IN, SSM_SL_ST)),
                  sl((SSM_SL_ST, SSM_SL_IN)), sl((SSM_SL_ST, SSM_SL_IN)),
                  sl((1, SSM_SL_ST)), sl((1, SSM_SL_ST)), sl((1, SSM_SL_IN)),
                  sl((nb, SSM_SL_ST)), sl((nb, SSM_SL_ST))],
        out_specs=(pl.BlockSpec((nt, nb, SSM_SL_IN), lambda s: (0, 0, s)),
                   sl((nb, SSM_SL_ST)), sl((nb, SSM_SL_ST))),
        out_shape=(jax.ShapeDtypeStruct((nt, nb, SSM_W), F32),
                   jax.ShapeDtypeStruct((SSM_SLICES, nb, SSM_SL_ST), F32),
                   jax.ShapeDtypeStruct((SSM_SLICES, nb, SSM_SL_ST), F32)),
        compiler_params=_cparams(("parallel",), 32),
        name="ssm_sample",
    )(proj3, bbr, bbi, cr, ci, abr, abi, dsk, h0r, h0i)


def _conv_prompt_kernel(h_ref, cb_ref, cc_ref, w_ref, buf_ref, cv_ref, nbuf_ref, cat, *, t):
    pad = 8
    cat[pad - 2:pad, :] = buf_ref[...]
    cat[pad:pad + t, :] = cc_ref[...] * h_ref[...]
    w = w_ref[...]
    y = (w[0:1] * cat[pad - 2:pad - 2 + t, :] + w[1:2] * cat[pad - 1:pad - 1 + t, :]
         + w[2:3] * cat[pad:pad + t, :])
    cv_ref[...] = cb_ref[...] * y
    nbuf_ref[...] = cat[pad + t - 2:pad + t, :]


def conv_prompt(proj3, h_col0, conv_w, buf):
    nb, t, _ = proj3.shape
    c0 = h_col0 // LANE
    nblk = CONV_W // LANE
    kern = functools.partial(_conv_prompt_kernel, t=t)
    col = lambda off: pl.BlockSpec((None, t, LANE), lambda b, c: (b, 0, c0 + off + c))
    return pl.pallas_call(
        kern,
        grid=(nb, nblk),
        in_specs=[col(0), col(nblk), col(2 * nblk),
                  pl.BlockSpec((CONV_TAPS, LANE), lambda b, c: (0, c)),
                  pl.BlockSpec((None, CONV_TAPS - 1, LANE), lambda b, c: (b, 0, c))],
        out_specs=(pl.BlockSpec((None, t, LANE), lambda b, c: (b, 0, c)),
                   pl.BlockSpec((None, CONV_TAPS - 1, LANE), lambda b, c: (b, 0, c))),
        out_shape=(jax.ShapeDtypeStruct((nb, t, CONV_W), F32),
                   jax.ShapeDtypeStruct((nb, CONV_TAPS - 1, CONV_W), F32)),
        scratch_shapes=[pltpu.VMEM((t + 8, LANE), F32)],
        compiler_params=_cparams(("parallel", "parallel"), 32),
        name="conv_prompt",
    )(proj3, proj3, proj3, conv_w, buf)


def _conv_sample_kernel(h_ref, cb_ref, cc_ref, w_ref, buf_ref, cv_ref, nbuf_ref, *, nt):
    w = w_ref[...]
    rows = [buf_ref[0], buf_ref[1]] + [cc_ref[t] * h_ref[t] for t in range(nt)]
    for t in range(nt):
        y = w[0:1] * rows[t] + w[1:2] * rows[t + 1] + w[2:3] * rows[t + 2]
        cv_ref[t] = cb_ref[t] * y
    nbuf_ref[0] = rows[nt]
    nbuf_ref[1] = rows[nt + 1]


def conv_sample(proj3, h_col0, conv_w, buf_t):
    nt, nb, _ = proj3.shape
    c0 = h_col0 // CONV_W
    kern = functools.partial(_conv_sample_kernel, nt=nt)
    col = lambda off: pl.BlockSpec((nt, nb, CONV_W), lambda i: (0, 0, c0 + off))
    return pl.pallas_call(
        kern,
        grid=(1,),
        in_specs=[col(0), col(1), col(2),
                  pl.BlockSpec((CONV_TAPS, CONV_W), lambda i: (0, 0)),
                  pl.BlockSpec((CONV_TAPS - 1, nb, CONV_W), lambda i: (0, 0, 0))],
        out_specs=(pl.BlockSpec((nt, nb, CONV_W), lambda i: (0, 0, 0)),
                   pl.BlockSpec((CONV_TAPS - 1, nb, CONV_W), lambda i: (0, 0, 0))),
        out_shape=(jax.ShapeDtypeStruct((nt, nb, CONV_W), F32),
                   jax.ShapeDtypeStruct((CONV_TAPS - 1, nb, CONV_W), F32)),
        compiler_params=_cparams(("arbitrary",), 32),
        name="conv_sample",
    )(proj3, proj3, proj3, conv_w, buf_t)


def _suffix_matrices(n):
    r = jnp.arange(n)[:, None]
    c = jnp.arange(n + LANE)[None, :]
    return jnp.where((r > c) | (c >= n), 1.0, 0.0).astype(BF16)


def _att_prompt_kernel(bias_ref, q_ref, k_ref, v_ref, maug_ref, mtail_ref, o_ref, kbf, vbf, acc, car, *, t):
    npair = N_HEADS // 2
    nblk = t // QBLK
    tail = t - nblk * QBLK
    kbf[...] = k_ref[...].astype(BF16)
    vbf[...] = v_ref[...].astype(BF16)
    pairs = range(npair)
    cols = [slice(p * LANE, (p + 1) * LANE) for p in pairs]

    def qblock(row0, nq, n_prev):
        lane = lax.broadcasted_iota(jnp.int32, (nq, LANE), 1)
        row = lax.broadcasted_iota(jnp.int32, (2 * nq, 1), 0)
        q2, bias2 = [], []
        for p in pairs:
            q = q_ref[pl.ds(row0, nq), cols[p]] * QK_SCALE
            q2.append(jnp.concatenate([jnp.where(lane < HEAD_DIM, q, 0.0),
                                       jnp.where(lane >= HEAD_DIM, q, 0.0)], axis=0).astype(BF16))
            bias2.append(jnp.where(row < nq, bias_ref[2 * p], bias_ref[2 * p + 1]))

        def step(r, nk, diag):
            nj = nk // KBLK
            zs = [_dot_nt(q2[p], kbf[pl.ds(r, nk), cols[p]]) + bias2[p] for p in pairs]
            sps = [_softplus(z) for z in zs]
            if diag:
                ri = lax.broadcasted_iota(jnp.int32, (2 * nq, nk), 0)
                ci = lax.broadcasted_iota(jnp.int32, (2 * nq, nk), 1)
                vis = ci < jnp.where(ri >= nq, ri - nq, ri)
                lks = [jnp.where(vis, -sp, 0.0) for sp in sps]
            else:
                lks = [-sp for sp in sps]
            m2 = jnp.concatenate([maug_ref[...], maug_ref[...]], axis=0)
            s_parts = []
            for lk in lks:
                hi, lo = _split_bf16(lk)
                s_parts.append([_dot(jnp.concatenate([hi[:, j * KBLK:(j + 1) * KBLK],
                                                      lo[:, j * KBLK:(j + 1) * KBLK]], axis=1), m2)
                                for j in range(nj)])
            ws, new_car = [], []
            for p in pairs:
                add = None if diag else car[p, 0:2 * nq, :]
                sufs = [None] * nj
                for j in reversed(range(nj)):
                    sj, tj = s_parts[p][j][:, :KBLK], s_parts[p][j][:, KBLK:]
                    sufs[j] = sj if add is None else sj + add
                    add = tj if add is None else add + tj
                suf = sufs[0] if nj == 1 else jnp.concatenate(sufs, axis=1)
                w = jnp.exp(zs[p] - sps[p] + suf)
                if diag:
                    w = jnp.where(vis, w, 0.0)
                ws.append(w.astype(BF16))
                new_car.append(add)
            for p in pairs:
                pv = _dot(ws[p], vbf[pl.ds(r, nk), cols[p]])
                if diag:
                    acc[p, 0:2 * nq, :] = pv
                else:
                    acc[p, 0:2 * nq, :] += pv
                car[p, 0:2 * nq, :] = new_car[p]

        def small_diag():
            ri = lax.broadcasted_iota(jnp.int32, (2 * nq, nq), 0)
            ci = lax.broadcasted_iota(jnp.int32, (2 * nq, nq), 1)
            vis = ci < jnp.where(ri >= nq, ri - nq, ri)
            zs = [_dot_nt(q2[p], kbf[pl.ds(row0, nq), cols[p]]) + bias2[p] for p in pairs]
            sps = [_softplus(z) for z in zs]
            hilo = [_split_bf16(jnp.where(vis, -sp, 0.0)) for sp in sps]
            s_augs = [_dot(hi, mtail_ref[...]) + _dot(lo, mtail_ref[...]) for hi, lo in hilo]
            ws = [jnp.where(vis, jnp.exp(zs[p] - sps[p] + s_augs[p][:, :nq]), 0.0).astype(BF16) for p in pairs]
            for p in pairs:
                acc[p, 0:2 * nq, :] = _dot(ws[p], vbf[pl.ds(row0, nq), cols[p]])
                car[p, 0:2 * nq, :] = s_augs[p][:, nq:]

        if nq % KBLK == 0:
            step(row0, nq, True)
        else:
            small_diag()

        def body(i, carry):
            step(pl.multiple_of((n_prev - 1 - i) * QBLK, QBLK), QBLK, False)
            return carry

        lax.fori_loop(0, n_prev, body, 0)
        for p in pairs:
            o2 = acc[p, 0:2 * nq, :]
            o_ref[pl.ds(row0, nq), cols[p]] = jnp.where(lane < HEAD_DIM, o2[0:nq], o2[nq:2 * nq])

    def outer(i, carry):
        qblock(pl.multiple_of(i * QBLK, QBLK), QBLK, i)
        return carry

    lax.fori_loop(0, nblk, outer, 0)
    if tail:
        qblock(nblk * QBLK, tail, nblk)


def attention_prompt(proj3, bias):
    nb, t, _ = proj3.shape
    tail = t % QBLK
    npair = N_HEADS // 2
    col = lambda i: pl.BlockSpec((None, t, ATT_WIDTH), lambda b: (b, 0, i))
    maug = _suffix_matrices(KBLK)
    mtail = _suffix_matrices(tail if tail else 8)
    kern = functools.partial(_att_prompt_kernel, t=t)
    return pl.pallas_call(
        kern,
        grid=(nb,),
        in_specs=[pl.BlockSpec(memory_space=pltpu.SMEM),
                  col(0), col(1), col(2),
                  pl.BlockSpec(maug.shape, lambda b: (0, 0)),
                  pl.BlockSpec(mtail.shape, lambda b: (0, 0))],
        out_specs=pl.BlockSpec((None, t, ATT_WIDTH), lambda b: (b, 0, 0)),
        out_shape=jax.ShapeDtypeStruct((nb, t, ATT_WIDTH), F32),
        scratch_shapes=[pltpu.VMEM((t, ATT_WIDTH), BF16), pltpu.VMEM((t, ATT_WIDTH), BF16),
                        pltpu.VMEM((npair, 2 * QBLK, LANE), F32), pltpu.VMEM((npair, 2 * QBLK, LANE), F32)],
        compiler_params=_cparams(("parallel",), 52),
        name="attention_prompt",
    )(bias, proj3, proj3, proj3, maug, mtail)


def _att_paged_kernel(pt_ref, q_ref, kn_ref, vn_ref, bias_ref, maug_ref, *refs, npages, ns):
    del pt_ref
    k_refs = refs[:npages]
    v_refs = refs[npages:2 * npages]
    o_ref = refs[2 * npages]
    nr = ns * N_HEADS
    q = q_ref[...] * QK_SCALE
    lane = lax.broadcasted_iota(jnp.int32, (N_HEADS, ATT_WIDTH), 1)
    hrow = lax.broadcasted_iota(jnp.int32, (N_HEADS, ATT_WIDTH), 0)
    hmask = (lane >= hrow * HEAD_DIM) & (lane < (hrow + 1) * HEAD_DIM)
    wq = jnp.concatenate(
        [jnp.where(hmask, jnp.broadcast_to(q[i:i + 1, :], (N_HEADS, ATT_WIDTH)), 0.0) for i in range(ns)],
        axis=0)
    wqb = wq.astype(BF16)
    bias = jnp.concatenate([bias_ref[...]] * ns, axis=0)
    bias1 = bias[:, 0:1]
    qidx = lax.broadcasted_iota(jnp.int32, (nr, 1), 0) // N_HEADS

    kn = kn_ref[...]
    vn = vn_ref[...]
    zs, sps, lks, vis = [], [], [], []
    for j in range(ns):
        zj = jnp.sum(wq * kn[j:j + 1, :], axis=-1, keepdims=True) + bias1
        vj = qidx > j
        spj = _softplus(zj)
        zs.append(zj); sps.append(spj); vis.append(vj)
        lks.append(jnp.where(vj, -spj, 0.0))
    o = jnp.zeros((nr, ATT_WIDTH), F32)
    suf = jnp.zeros((nr, 1), F32)
    for j in reversed(range(ns)):
        wj = jnp.where(vis[j], jnp.exp(zs[j] - sps[j] + suf), 0.0)
        o = o + wj * vn[j:j + 1, :]
        suf = suf + lks[j]
    car = jnp.broadcast_to(suf, (nr, LANE))

    pages = range(npages)
    zs = [_dot(wqb, k_refs[p][...].astype(BF16)) + bias for p in pages]
    sps = [_softplus(z) for z in zs]
    hilo = [_split_bf16(-sp) for sp in sps]
    hi_st = jnp.concatenate([h for h, _ in hilo], axis=0)
    lo_st = jnp.concatenate([l for _, l in hilo], axis=0)
    s_st = _dot(hi_st, maug_ref[...]) + _dot(lo_st, maug_ref[...])
    cars = [None] * npages
    for p in reversed(pages):
        cars[p] = car
        car = car + s_st[p * nr:(p + 1) * nr, KBLK:]
    for p in pages:
        w = jnp.exp(zs[p] - sps[p] + s_st[p * nr:(p + 1) * nr, :KBLK] + cars[p])
        o = o + _dot_nt(w.astype(BF16), v_refs[p][...].astype(BF16))

    o3 = o.reshape(ns, N_HEADS, ATT_WIDTH)
    o_ref[...] = jnp.sum(jnp.where(hmask[None], o3, 0.0), axis=1)


def attention_paged(q, kn, vn, bias, cache_k, cache_v, layer, page_table):
    nb, ns, _ = q.shape
    npages = page_table.shape[1]
    maug = _suffix_matrices(KBLK)
    bias_rep = jnp.broadcast_to(bias[:, None], (N_HEADS, LANE))
    row = pl.BlockSpec((None, ns, ATT_WIDTH), lambda b, pt: (b, 0, 0))

    def page_spec(p):
        return pl.BlockSpec((None, None, ATT_WIDTH, PAGE), lambda b, pt: (layer, pt[b, p], 0, 0))

    kern = functools.partial(_att_paged_kernel, npages=npages, ns=ns)
    return pl.pallas_call(
        kern,
        grid_spec=pltpu.PrefetchScalarGridSpec(
            num_scalar_prefetch=1,
            grid=(nb,),
            in_specs=[row, row, row,
                      pl.BlockSpec((N_HEADS, LANE), lambda b, pt: (0, 0)),
                      pl.BlockSpec(maug.shape, lambda b, pt: (0, 0))]
                     + [page_spec(p) for p in range(npages)] * 2,
            out_specs=row),
        out_shape=jax.ShapeDtypeStruct((nb, ns, ATT_WIDTH), F32),
        compiler_params=_cparams(("arbitrary",), 40),
        name="attention_paged",
    )(page_table, q, kn, vn, bias_rep, maug, *([cache_k] * npages), *([cache_v] * npages))


def _mix_kernel(x_ref, att_ref, ssm_ref, cv_ref, gate_ref, wa_ref, wg_ref, wc_ref, wo_ref,
                g_ref, b_ref, o_ref, *, d):
    y_att = _dot(att_ref[...].astype(BF16), wa_ref[...])
    zg = _dot(ssm_ref[...].astype(BF16), wg_ref[...])
    y_ssm = zg[:, :d] * _sigmoid(zg[:, d:])
    y_conv = _dot(cv_ref[...].astype(BF16), wc_ref[...])
    mix = (_sigmoid(gate_ref[:, 0:d]) * y_att + _sigmoid(gate_ref[:, d:2 * d]) * y_ssm
           + _sigmoid(gate_ref[:, 2 * d:3 * d]) * y_conv)
    r = ALPHA * x_ref[...] + _dot(mix.astype(BF16), wo_ref[...])
    o_ref[...] = _layer_norm(r, g_ref[...], b_ref[...])


def mix_block(x, att, ssm, cv, gates, wa, wg, wc, wo, g, b, layer, tm):
    n, d = x.shape
    rows = lambda w: pl.BlockSpec((tm, w), lambda i: (i, 0))
    lay = lambda a: pl.BlockSpec((None,) + a.shape[1:], lambda i: (layer, 0, 0))
    kern = functools.partial(_mix_kernel, d=d)
    return pl.pallas_call(
        kern,
        grid=(n // tm,),
        in_specs=[rows(d), rows(att.shape[1]), rows(ssm.shape[1]), rows(cv.shape[1]), rows(gates.shape[1]),
                  lay(wa), lay(wg), lay(wc), lay(wo), lay(g), lay(b)],
        out_specs=rows(d),
        out_shape=jax.ShapeDtypeStruct((n, d), F32),
        compiler_params=_cparams(("parallel",), 52),
        name="mix_block",
    )(x, att, ssm, cv, gates, wa, wg, wc, wo, g, b)


def _mlp_kernel(x_ref, wu_ref, wd_ref, g_ref, b_ref, o_ref, acc_ref, xb_ref):
    c = pl.program_id(1)

    @pl.when(c == 0)
    def _():
        acc_ref[...] = jnp.zeros_like(acc_ref)
        xb_ref[...] = x_ref[...].astype(BF16)

    h = jnp.maximum(_dot(xb_ref[...], wu_ref[...]), 0.0)
    acc_ref[...] += _dot((h * h).astype(BF16), wd_ref[...])

    @pl.when(c == pl.num_programs(1) - 1)
    def _():
        o_ref[...] = _layer_norm(ALPHA * x_ref[...] + acc_ref[...], g_ref[...], b_ref[...])


def mlp_block(x, wu, wd, g, b, layer, tm, tf):
    n, d = x.shape
    f = wu.shape[2]
    return pl.pallas_call(
        _mlp_kernel,
        grid=(n // tm, f // tf),
        in_specs=[pl.BlockSpec((tm, d), lambda i, c: (i, 0)),
                  pl.BlockSpec((None, d, tf), lambda i, c: (layer, 0, c)),
                  pl.BlockSpec((None, tf, d), lambda i, c: (layer, c, 0)),
                  pl.BlockSpec((None, 1, d), lambda i, c: (layer, 0, 0)),
                  pl.BlockSpec((None, 1, d), lambda i, c: (layer, 0, 0))],
        out_specs=pl.BlockSpec((tm, d), lambda i, c: (i, 0)),
        out_shape=jax.ShapeDtypeStruct((n, d), F32),
        scratch_shapes=[pltpu.VMEM((tm, d), F32), pltpu.VMEM((tm, d), BF16)],
        compiler_params=_cparams(("parallel", "arbitrary"), 48),
        name="mlp_block",
    )(x, wu, wd, g, b)


def _row_tile(n, target):
    best = 8
    for c in range(8, target + 1, 8):
        if n % c == 0:
            best = c
    return best


def _slice_state(h):
    nb = h.shape[0]
    return h.reshape(nb, SSM_SLICES, SSM_SL_ST).transpose(1, 0, 2)


def _unslice_state(h):
    nb = h.shape[1]
    return h.transpose(1, 0, 2).reshape(nb, SSM_NGRP, SSM_NSTATE)


def kernel(x_prompt, x_sample, cache_k, cache_v, state_ssm_re, state_ssm_im, state_conv, page_table, meta_tokens, ln_in_g, ln_in_b, w_in, sb_bias, ssm_lam_re, ssm_lam_im, ssm_log_dt, ssm_b_re, ssm_b_im, ssm_c_re, ssm_c_im, ssm_d, w_glu, conv_w, w_branch_a, w_branch_c, w_out, ln1_g, ln1_b, w_up, w_down, ln2_g, ln2_b):
    nb, seq, d = x_prompt.shape
    sb, ss, _ = x_sample.shape
    t = seq + N_META_TOK
    n_p, n_s = nb * t, sb * ss
    depth = w_in.shape[0]

    w_main = w_in[:, :, :MAIN_W].astype(BF16)
    w_gate = w_in[:, :, MAIN_W:].astype(BF16)
    wa, wg, wc, wo = (w.astype(BF16) for w in (w_branch_a, w_glu, w_branch_c, w_out))
    wu, wd = w_up.astype(BF16), w_down.astype(BF16)

    abr, abi, bbr_t, bbi_t = ssm_prep(ssm_lam_re, ssm_lam_im, ssm_log_dt, ssm_b_re, ssm_b_im)
    abr = abr.reshape(depth, SSM_SLICES, 1, SSM_SL_ST)
    abi = abi.reshape(depth, SSM_SLICES, 1, SSM_SL_ST)
    bbr_t = bbr_t.reshape(depth, SSM_NGRP, SSM_GRP, SSM_NSTATE)
    bbi_t = bbi_t.reshape(depth, SSM_NGRP, SSM_GRP, SSM_NSTATE)
    dsk = ssm_d.reshape(depth, SSM_SLICES, 1, SSM_SL_IN)
    cache_k4 = cache_k.transpose(0, 1, 3, 4, 2).reshape(cache_k.shape[0], cache_k.shape[1], ATT_WIDTH, PAGE)
    cache_v4 = cache_v.transpose(0, 1, 3, 4, 2).reshape(cache_v.shape[0], cache_v.shape[1], ATT_WIDTH, PAGE)

    tm_p = _row_tile(n_p, 688)
    tm_mix = _row_tile(n_p, 344)
    tm_s = _row_tile(n_s, 256)
    tc = _row_tile(t, 344)

    meta = jnp.broadcast_to(meta_tokens.astype(x_prompt.dtype)[None], (nb, N_META_TOK, d))
    xp = layernorm_rows(jnp.concatenate([meta, x_prompt], axis=1).reshape(n_p, d), ln_in_g, ln_in_b, tm_p)
    xs = layernorm_rows(x_sample.transpose(1, 0, 2).reshape(n_s, d), ln_in_g, ln_in_b, tm_s)

    h0_zero = jnp.zeros((SSM_SLICES, nb, SSM_SL_ST), F32)
    buf_zero = jnp.zeros((nb, CONV_TAPS - 1, CONV_W), F32)
    outs = [[] for _ in range(10)]
    u_col, h_col = 3 * ATT_WIDTH, 3 * ATT_WIDTH + SSM_W
    bd = jax.vmap(_block_diag_slices)
    bbr_all, bbi_all = bd(bbr_t), bd(bbi_t)
    cr_all = bd(jnp.swapaxes(ssm_c_re, -1, -2))
    ci_all = bd(jnp.swapaxes(ssm_c_im, -1, -2))
    ln1 = (ln1_g.reshape(depth, 1, d), ln1_b.reshape(depth, 1, d))
    ln2 = (ln2_g.reshape(depth, 1, d), ln2_b.reshape(depth, 1, d))
    for l in range(depth):
        ssm_w = (bbr_all[l], bbi_all[l], cr_all[l], ci_all[l], abr[l], abi[l], dsk[l])

        main = matmul_xw(xp, w_main, l, tm_p, MAIN_W // 2).reshape(nb, t, MAIN_W)
        gates = matmul_xw(xp, w_gate, l, tm_p, (3 * d) // 2)
        att = attention_prompt(main, sb_bias[l])
        ysm, htr, hti = ssm_prompt(main, u_col, *ssm_w, h0_zero, h0_zero, tc)
        cv, nbuf = conv_prompt(main, h_col, conv_w[l], buf_zero)
        x1 = mix_block(xp, att.reshape(n_p, -1), ysm.reshape(n_p, -1), cv.reshape(n_p, -1), gates,
                       wa, wg, wc, wo, *ln1, l, tm_mix)
        xp = mlp_block(x1, wu, wd, *ln2, l, tm_p, 1024)
        outs[0].append(main[:, :, ATT_WIDTH:2 * ATT_WIDTH].reshape(nb, t, N_HEADS, HEAD_DIM))
        outs[1].append(main[:, :, 2 * ATT_WIDTH:3 * ATT_WIDTH].reshape(nb, t, N_HEADS, HEAD_DIM))
        outs[4].append(_unslice_state(htr))
        outs[5].append(_unslice_state(hti))
        outs[8].append(nbuf)

        main_s = matmul_xw(xs, w_main, l, tm_s, MAIN_W // 2).reshape(ss, sb, MAIN_W)
        gates_s = matmul_xw(xs, w_gate, l, tm_s, (3 * d) // 2)
        qkv = main_s[:, :, :3 * ATT_WIDTH].transpose(1, 0, 2)
        q_s, k_s, v_s = (qkv[:, :, i * ATT_WIDTH:(i + 1) * ATT_WIDTH] for i in range(3))
        att_s = attention_paged(q_s, k_s, v_s, sb_bias[l], cache_k4, cache_v4, l, page_table)
        att_s = att_s.transpose(1, 0, 2).reshape(n_s, ATT_WIDTH)
        ysm_s, htr_s, hti_s = ssm_sample(main_s, u_col, *ssm_w,
                                         _slice_state(state_ssm_re[l]), _slice_state(state_ssm_im[l]))
        cv_s, nbuf_s = conv_sample(main_s, h_col, conv_w[l], state_conv[l].transpose(1, 0, 2))
        x1_s = mix_block(xs, att_s, ysm_s.reshape(n_s, -1), cv_s.reshape(n_s, -1), gates_s,
                         wa, wg, wc, wo, *ln1, l, tm_s)
        xs = mlp_block(x1_s, wu, wd, *ln2, l, tm_s, 1024)
        outs[2].append(k_s.reshape(sb, ss, N_HEADS, HEAD_DIM))
        outs[3].append(v_s.reshape(sb, ss, N_HEADS, HEAD_DIM))
        outs[6].append(_unslice_state(htr_s))
        outs[7].append(_unslice_state(hti_s))
        outs[9].append(nbuf_s.transpose(1, 0, 2))

    y_prompt = xp.reshape(nb, t, d)[:, N_META_TOK:]
    y_sample = xs.reshape(ss, sb, d).transpose(1, 0, 2)
    return (y_prompt, y_sample) + tuple(jnp.stack(o) for o in outs)
```
